```python
import jax, jax.numpy as jnp
from jax import lax
import numpy as np

D_MODEL = 1024
BATCH = 8
SEQ = 8192
DEPTH = 2

GRID_W = 64
CTX_LEN = 256
HEAD_DIM = 64
M_WIDTH = D_MODEL // 4
M_HEADS = M_WIDTH // HEAD_DIM
M_CHUNK = 64
N_DIR = 2
A_WIDTH = D_MODEL // 2
A_HEADS = A_WIDTH // HEAD_DIM
A_KV_HEADS = A_HEADS // 4
A_GROUP = A_HEADS // A_KV_HEADS
WINDOW = 128
Q_BLOCK = 128
ROPE_THETA = 10000.0
C_WIDTH = D_MODEL // 4
CONV_K = 31
MIX_WIDTH = M_WIDTH + A_WIDTH + C_WIDTH
FFN_HIDDEN = ((8 * D_MODEL + 3 * 256 - 1) // (3 * 256)) * 256
IN_SPLITS = (M_WIDTH, M_WIDTH, M_WIDTH, M_WIDTH, N_DIR * 2 * M_HEADS,
             A_WIDTH, A_KV_HEADS * HEAD_DIM, A_KV_HEADS * HEAD_DIM, 2 * C_WIDTH)
IN_COLS = sum(IN_SPLITS)
EPS = 1e-6
NEG_INF = -1e30

kernel_name = 'hybrid_mlstm_swa_conformer_dit'


def rmsnorm(x, g):
    xf = x.astype(jnp.float32)
    y = xf * lax.rsqrt(jnp.mean(xf * xf, axis=-1, keepdims=True) + EPS)
    return (y * g.astype(jnp.float32)).astype(x.dtype)


def modulate(h, shift, scale):
    return h * (1 + scale) + shift


def split_cols(p):
    idx = np.cumsum(IN_SPLITS)[:-1].tolist()
    return jnp.split(p, idx, axis=-1)


def heads(a, n):
    return a.reshape(a.shape[:2] + (n, HEAD_DIM))


def rope_axis(x, pos):
    half = x.shape[-1] // 2
    freqs = ROPE_THETA ** (-jnp.arange(half, dtype=jnp.float32) / half)
    ang = pos.astype(jnp.float32)[:, None] * freqs[None, :]
    cos = jnp.cos(ang)[None, :, None, :]
    sin = jnp.sin(ang)[None, :, None, :]
    xf = x.astype(jnp.float32)
    x1, x2 = xf[..., :half], xf[..., half:]
    return jnp.concatenate([x1 * cos - x2 * sin, x2 * cos + x1 * sin], axis=-1).astype(x.dtype)


def rope_2d(x, rows, cols):
    d = x.shape[-1] // 2
    return jnp.concatenate([rope_axis(x[..., :d], rows), rope_axis(x[..., d:], cols)], axis=-1)


def mlstm_scan(q, k, v, li, lf, state):
    b_, nh, L, dh = q.shape
    nc = L // M_CHUNK

    def chunks(a):
        return jnp.moveaxis(a.reshape(a.shape[:2] + (nc, M_CHUNK) + a.shape[3:]), 2, 0)

    tril = jnp.tril(jnp.ones((M_CHUNK, M_CHUNK), dtype=bool))

    def step(carry, inp):
        C, n, m = carry
        qc, kc, vc, lic, lfc = inp
        bcum = jnp.cumsum(lfc, axis=-1)
        dmat = jnp.where(tril, bcum[..., :, None] - bcum[..., None, :] + lic[..., None, :], -jnp.inf)
        inter = bcum + m[..., None]
        m_t = jnp.maximum(inter, jnp.max(dmat, axis=-1))
        s = jnp.einsum('bhtd,bhsd->bhts', qc, kc) * jnp.exp(dmat - m_t[..., None])
        w = jnp.exp(inter - m_t)
        num = w[..., None] * jnp.einsum('bhed,bhtd->bhte', C, qc) + jnp.einsum('bhts,bhse->bhte', s, vc)
        den = w * jnp.einsum('bhd,bhtd->bht', n, qc) + jnp.sum(s, axis=-1)
        h = num / jnp.maximum(jnp.abs(den), jnp.exp(-m_t))[..., None]
        b_end = bcum[..., -1]
        g = b_end[..., None] - bcum + lic
        m_new = jnp.maximum(b_end + m, jnp.max(g, axis=-1))
        decay = jnp.exp(b_end + m - m_new)
        wg = jnp.exp(g - m_new[..., None])
        C_new = decay[..., None, None] * C + jnp.einsum('bhs,bhse,bhsd->bhed', wg, vc, kc)
        n_new = decay[..., None] * n + jnp.einsum('bhs,bhsd->bhd', wg, kc)
        return (C_new, n_new, m_new), h

    state, h = lax.scan(step, state, tuple(chunks(a) for a in (q, k, v, li, lf)))
    h = jnp.moveaxis(h, 0, 2).reshape(b_, nh, L, dh)
    return h, state


def mlstm_prep(q, k, v, g, gate_bias):
    b_, L = q.shape[0], q.shape[1]

    def to_heads(a):
        return jnp.swapaxes(a.astype(jnp.float32).reshape(b_, L, M_HEADS, HEAD_DIM), 1, 2)

    gg = g.astype(jnp.float32).reshape(b_, L, N_DIR, 2, M_HEADS) + gate_bias.astype(jnp.float32)
    gg = jnp.transpose(gg, (2, 3, 0, 4, 1))
    li = gg[:, 0]
    lf = jax.nn.log_sigmoid(gg[:, 1])
    return to_heads(q), to_heads(k) * HEAD_DIM ** -0.5, to_heads(v), li, lf


def mlstm_bidir(xs, cs):
    qx, kx, vx, lix, lfx = xs
    qc, kc, vc, lic, lfc = cs
    b_ = qx.shape[0]
    zero = (jnp.zeros((b_, M_HEADS, HEAD_DIM, HEAD_DIM), jnp.float32),
            jnp.zeros((b_, M_HEADS, HEAD_DIM), jnp.float32),
            jnp.zeros((b_, M_HEADS), jnp.float32))
    hx, hc = None, None
    for d in range(N_DIR):
        fl = (lambda a: jnp.flip(a, axis=2)) if d == 1 else (lambda a: a)
        h_c, st = mlstm_scan(fl(qc), fl(kc), fl(vc), fl(lic[d]), fl(lfc[d]), zero)
        h_x, _ = mlstm_scan(fl(qx), fl(kx), fl(vx), fl(lix[d]), fl(lfx[d]), st)
        hx = fl(h_x) if hx is None else hx + fl(h_x)
        hc = fl(h_c) if hc is None else hc + fl(h_c)
    return hx, hc


def mlstm_out(h, o, gain):
    hn = h * lax.rsqrt(jnp.mean(h * h, axis=-1, keepdims=True) + EPS)
    b_, nh, L, dh = h.shape
    hn = jnp.swapaxes(hn, 1, 2).reshape(b_, L, nh * dh)
    return (hn * gain.astype(jnp.float32) * jax.nn.sigmoid(o.astype(jnp.float32))).astype(o.dtype)


def window_attention(q, k, v, kc, vc, sink):
    b_, S, H, dh = q.shape
    n_ctx = kc.shape[1]
    nb = S // Q_BLOCK
    span = Q_BLOCK + 2 * WINDOW
    pad = ((0, 0), (WINDOW, WINDOW), (0, 0), (0, 0))
    kp = jnp.pad(k, pad)
    vp = jnp.pad(v, pad)
    qg = q.reshape(b_, S, A_KV_HEADS, A_GROUP, dh)
    sink_g = sink.astype(jnp.float32).reshape(A_KV_HEADS, A_GROUP)[None, :, :, None, None]
    scale = dh ** -0.5

    def block(i):
        start = i * Q_BLOCK
        qb = lax.dynamic_slice_in_dim(qg, start, Q_BLOCK, axis=1)
        kb = lax.dynamic_slice_in_dim(kp, start, span, axis=1)
        vb = lax.dynamic_slice_in_dim(vp, start, span, axis=1)
        qpos = start + jnp.arange(Q_BLOCK)
        kpos = start - WINDOW + jnp.arange(span)
        mask = (jnp.abs(qpos[:, None] - kpos[None, :]) <= WINDOW) & (kpos >= 0)[None, :] & (kpos < S)[None, :]
        s_loc = jnp.einsum('bqhgd,bnhd->bhgqn', qb, kb).astype(jnp.float32) * scale
        s_loc = jnp.where(mask, s_loc, NEG_INF)
        s_ctx = jnp.einsum('bqhgd,bchd->bhgqc', qb, kc).astype(jnp.float32) * scale
        s_sink = jnp.broadcast_to(sink_g, s_ctx.shape[:-1] + (1,))
        p = jax.nn.softmax(jnp.concatenate([s_loc, s_ctx, s_sink], axis=-1), axis=-1).astype(v.dtype)
        o = (jnp.einsum('bhgqn,bnhd->bqhgd', p[..., :span], vb)
             + jnp.einsum('bhgqc,bchd->bqhgd', p[..., span:span + n_ctx], vc))
        return o.reshape(b_, Q_BLOCK, H * dh)

    out = lax.map(block, jnp.arange(nb))
    return jnp.moveaxis(out, 0, 1).reshape(b_, S, H * dh)


def context_attention(q, kc, vc, sink):
    b_, n_ctx, H, dh = q.shape
    qg = q.reshape(b_, n_ctx, A_KV_HEADS, A_GROUP, dh)
    s = jnp.einsum('bqhgd,bchd->bhgqc', qg, kc).astype(jnp.float32) * dh ** -0.5
    sink_g = sink.astype(jnp.float32).reshape(A_KV_HEADS, A_GROUP)[None, :, :, None, None]
    s_sink = jnp.broadcast_to(sink_g, s.shape[:-1] + (1,))
    p = jax.nn.softmax(jnp.concatenate([s, s_sink], axis=-1), axis=-1).astype(vc.dtype)
    o = jnp.einsum('bhgqc,bchd->bqhgd', p[..., :n_ctx], vc)
    return o.reshape(b_, n_ctx, H * dh)


def conformer_conv(u, dw_w, dw_b, ln_g, ln_b, pw_w):
    a, gate = jnp.split(u, 2, axis=-1)
    y = a * jax.nn.sigmoid(gate)
    y = lax.conv_general_dilated(y, dw_w[:, None, :].astype(y.dtype), window_strides=(1,),
                                 padding=[(CONV_K // 2, CONV_K // 2)],
                                 dimension_numbers=('NWC', 'WIO', 'NWC'),
                                 feature_group_count=C_WIDTH) + dw_b
    yf = y.astype(jnp.float32)
    mu = jnp.mean(yf, axis=-1, keepdims=True)
    var = jnp.mean(jnp.square(yf - mu), axis=-1, keepdims=True)
    y = ((yf - mu) * lax.rsqrt(var + EPS) * ln_g.astype(jnp.float32) + ln_b.astype(jnp.float32)).astype(u.dtype)
    return jax.nn.silu(y) @ pw_w


def swiglu(h, w_in, w_out):
    g, u = jnp.split(h @ w_in, 2, axis=-1)
    return (jax.nn.silu(g) * u) @ w_out


def setup_inputs(seed: int = 0) -> dict:
    key = jax.random.key(seed)
    ks = jax.random.split(key, 21)

    def nrm(k, shape, s):
        return jax.random.normal(k, shape, jnp.float32) * s

    x = nrm(ks[0], (BATCH, SEQ, D_MODEL), 1.0)
    c = nrm(ks[1], (BATCH, D_MODEL), 1.0)
    ctx = nrm(ks[2], (BATCH, CTX_LEN, D_MODEL), 1.0)
    c_ctx = nrm(ks[3], (D_MODEL,), 1.0)
    w_mod = nrm(ks[4], (DEPTH, D_MODEL, 6 * D_MODEL), 0.5 * D_MODEL ** -0.5)
    b_mod = nrm(ks[5], (DEPTH, 6 * D_MODEL), 0.02)
    norm_gain = 1.0 + nrm(ks[6], (DEPTH, 2, D_MODEL), 0.02)
    w_in = nrm(ks[7], (DEPTH, D_MODEL, IN_COLS), D_MODEL ** -0.5)
    i_bias = nrm(ks[8], (DEPTH, N_DIR, M_HEADS), 0.1)
    f_bias = jnp.linspace(3.0, 6.0, M_HEADS, dtype=jnp.float32) + nrm(ks[9], (DEPTH, N_DIR, M_HEADS), 0.1)
    mlstm_gate_bias = jnp.stack([i_bias, f_bias], axis=2)
    mlstm_head_gain = 1.0 + nrm(ks[10], (DEPTH, M_WIDTH), 0.02)
    attn_sink = nrm(ks[11], (DEPTH, A_HEADS), 0.5)
    conv_dw_w = nrm(ks[12], (DEPTH, CONV_K, C_WIDTH), CONV_K ** -0.5)
    conv_dw_b = nrm(ks[13], (DEPTH, C_WIDTH), 0.02)
    conv_ln_g = 1.0 + nrm(ks[14], (DEPTH, C_WIDTH), 0.02)
    conv_ln_b = nrm(ks[15], (DEPTH, C_WIDTH), 0.02)
    conv_pw_w = nrm(ks[16], (DEPTH, C_WIDTH, C_WIDTH), C_WIDTH ** -0.5)
    w_out = nrm(ks[17], (DEPTH, MIX_WIDTH, D_MODEL), MIX_WIDTH ** -0.5)
    w_ffn_in = nrm(ks[18], (DEPTH, D_MODEL, 2 * FFN_HIDDEN), D_MODEL ** -0.5)
    w_ffn_out = nrm(ks[19], (DEPTH, FFN_HIDDEN, D_MODEL), FFN_HIDDEN ** -0.5)
    final_gain = 1.0 + nrm(ks[20], (D_MODEL,), 0.02)
    return {'x': x, 'c': c, 'ctx': ctx, 'c_ctx': c_ctx, 'w_mod': w_mod, 'b_mod': b_mod,
            'norm_gain': norm_gain, 'w_in': w_in, 'mlstm_gate_bias': mlstm_gate_bias,
            'mlstm_head_gain': mlstm_head_gain, 'attn_sink': attn_sink, 'conv_dw_w': conv_dw_w,
            'conv_dw_b': conv_dw_b, 'conv_ln_g': conv_ln_g, 'conv_ln_b': conv_ln_b,
            'conv_pw_w': conv_pw_w, 'w_out': w_out, 'w_ffn_in': w_ffn_in, 'w_ffn_out': w_ffn_out,
            'final_gain': final_gain}


def reference(x, c, ctx, c_ctx, w_mod, b_mod, norm_gain, w_in, mlstm_gate_bias, mlstm_head_gain,
              attn_sink, conv_dw_w, conv_dw_b, conv_ln_g, conv_ln_b, conv_pw_w, w_out, w_ffn_in,
              w_ffn_out, final_gain):
    S = x.shape[1]
    n_rows = S // GRID_W
    rows = jnp.repeat(jnp.arange(n_rows), GRID_W, total_repeat_length=S)
    cols = jnp.arange(S) % GRID_W
    for l in range(DEPTH):
        last = l == DEPTH - 1
        mod_x = (jax.nn.silu(c) @ w_mod[l] + b_mod[l])[:, None, :]
        sh1x, sc1x, g1x, sh2x, sc2x, g2x = jnp.split(mod_x, 6, axis=-1)
        mod_c = jax.nn.silu(c_ctx) @ w_mod[l] + b_mod[l]
        sh1c, sc1c, g1c, sh2c, sc2c, g2c = jnp.split(mod_c, 6, axis=-1)

        hx = modulate(rmsnorm(x, norm_gain[l, 0]), sh1x, sc1x)
        hc = modulate(rmsnorm(ctx, norm_gain[l, 0]), sh1c, sc1c)
        px = split_cols(hx @ w_in[l])
        pc = split_cols(hc @ w_in[l])

        m_hx, m_hc = mlstm_bidir(mlstm_prep(px[0], px[1], px[2], px[4], mlstm_gate_bias[l]),
                                 mlstm_prep(pc[0], pc[1], pc[2], pc[4], mlstm_gate_bias[l]))
        m_x = mlstm_out(m_hx, px[3], mlstm_head_gain[l])

        q_x = rope_2d(heads(px[5], A_HEADS), rows, cols)
        k_x = rope_2d(heads(px[6], A_KV_HEADS), rows, cols)
        v_x = heads(px[7], A_KV_HEADS)
        k_c = heads(pc[6], A_KV_HEADS)
        v_c = heads(pc[7], A_KV_HEADS)
        a_x = window_attention(q_x, k_x, v_x, k_c, v_c, attn_sink[l])

        c_x = conformer_conv(px[8], conv_dw_w[l], conv_dw_b[l], conv_ln_g[l], conv_ln_b[l], conv_pw_w[l])

        x = x + g1x * (jnp.concatenate([m_x, a_x, c_x], axis=-1) @ w_out[l])
        x = x + g2x * swiglu(modulate(rmsnorm(x, norm_gain[l, 1]), sh2x, sc2x), w_ffn_in[l], w_ffn_out[l])

        if not last:
            m_c = mlstm_out(m_hc, pc[3], mlstm_head_gain[l])
            a_c = context_attention(heads(pc[5], A_HEADS), k_c, v_c, attn_sink[l])
            c_c = conformer_conv(pc[8], conv_dw_w[l], conv_dw_b[l], conv_ln_g[l], conv_ln_b[l], conv_pw_w[l])
            ctx = ctx + g1c * (jnp.concatenate([m_c, a_c, c_c], axis=-1) @ w_out[l])
            ctx = ctx + g2c * swiglu(modulate(rmsnorm(ctx, norm_gain[l, 1]), sh2c, sc2c), w_ffn_in[l], w_ffn_out[l])
    return rmsnorm(x, final_gain)
```

```python
import functools

import numpy as np
import jax
import jax.numpy as jnp
from jax import lax
from jax.experimental import pallas as pl
from jax.experimental.pallas import tpu as pltpu

F32 = jnp.float32
BF16 = jnp.bfloat16

D_MODEL = 1024
DEPTH = 2
GRID_W = 64
HEAD_DIM = 64
M_WIDTH = D_MODEL // 4
M_HEADS = M_WIDTH // HEAD_DIM
N_DIR = 2
A_WIDTH = D_MODEL // 2
A_HEADS = A_WIDTH // HEAD_DIM
A_KV_HEADS = A_HEADS // 4
A_GROUP = A_HEADS // A_KV_HEADS
KV_WIDTH = A_KV_HEADS * HEAD_DIM
WINDOW = 128
ROPE_THETA = 10000.0
C_WIDTH = D_MODEL // 4
CONV_K = 31
CONV_HALO = 16
FFN_HIDDEN = ((8 * D_MODEL + 3 * 256 - 1) // (3 * 256)) * 256
IN_SPLITS = (M_WIDTH, M_WIDTH, M_WIDTH, M_WIDTH, N_DIR * 2 * M_HEADS,
             A_WIDTH, KV_WIDTH, KV_WIDTH, 2 * C_WIDTH)
N_GATES = N_DIR * 2 * M_HEADS
N_CHAINS = N_DIR * M_HEADS
EPS = 1e-6
NEG_INF = -1e30

LANES = 128
VMEM_LIMIT = 52 * 1024 * 1024

C_MQ = 0
C_MK = C_MQ + M_WIDTH
C_MV = C_MK + M_WIDTH
C_MO = C_MV + M_WIDTH
C_AQ = C_MO + M_WIDTH
C_AK = C_AQ + A_WIDTH
C_AV = C_AK + KV_WIDTH
C_CU = C_AV + KV_WIDTH
C_LF = C_CU + 2 * C_WIDTH
N_COLS = C_LF + LANES
NT_ROWS = M_WIDTH + N_GATES
STATE_ROWS = HEAD_DIM + 8


def _cparams(sem):
    return pltpu.CompilerParams(dimension_semantics=sem, vmem_limit_bytes=VMEM_LIMIT)


def _const_spec(shape):
    nd = len(shape)
    return pl.BlockSpec(shape, lambda *_: (0,) * nd, pipeline_mode=pl.Buffered(1))


def _log_sigmoid(x):
    return jnp.minimum(x, 0.0) - jnp.log1p(jnp.exp(-jnp.abs(x)))


def _sigmoid(x):
    return jax.nn.sigmoid(x)


def _rms(x):
    return x * lax.rsqrt(jnp.mean(x * x, axis=-1, keepdims=True) + EPS)


def _dot(a, b):
    return jnp.dot(a, b, preferred_element_type=F32)


def _dot_nt(a, b):
    return lax.dot_general(a, b, (((1,), (1,)), ((), ())), preferred_element_type=F32)


def _split_bf16(x, parts):
    out = []
    r = x
    for _ in range(parts):
        p = r.astype(BF16)
        out.append(p)
        r = r - p.astype(F32)
    return out


def _mod_kernel(c_ref, w_ref, b_ref, o_ref):
    cc = c_ref[...]
    s = (cc * _sigmoid(cc)).astype(BF16)
    o_ref[0] = _dot(s, w_ref[0].astype(BF16)) + b_ref[0]


def _modulation(cc, w_mod, b_mod):
    rows = cc.shape[0]
    n = w_mod.shape[-1]
    tn = 1536
    return pl.pallas_call(
        _mod_kernel,
        grid=(DEPTH, n // tn),
        in_specs=[pl.BlockSpec((rows, D_MODEL), lambda l, j: (0, 0)),
                  pl.BlockSpec((1, D_MODEL, tn), lambda l, j: (l, 0, j)),
                  pl.BlockSpec((1, 1, tn), lambda l, j: (l, 0, j))],
        out_specs=pl.BlockSpec((1, rows, tn), lambda l, j: (l, 0, j)),
        out_shape=jax.ShapeDtypeStruct((DEPTH, rows, n), F32),
        compiler_params=_cparams(("arbitrary", "arbitrary")),
        name="modulation",
    )(cc, w_mod, b_mod.reshape(DEPTH, 1, n))


def _inproj_kernel(x_ref, gain_ref, shift_ref, scale_ref, w_ref, wt_ref, bcol_ref, brow_ref,
                   cos_ref, sin_ref,
                   mq_ref, mk_ref, mv_ref, mo_ref, vt_ref, gt_ref, lfc_ref, aq_ref, ak_ref, av_ref, cu_ref,
                   *, rope):
    xf = x_ref[0]
    h = _rms(xf) * gain_ref[...]
    h = h * (1.0 + scale_ref[0]) + shift_ref[0]
    hb = h.astype(BF16)
    p = _dot(hb, w_ref[...])
    pt = _dot_nt(wt_ref[...], hb)
    mq_ref[0] = p[:, C_MQ:C_MK]
    mk_ref[0] = p[:, C_MK:C_MV]
    mv_ref[0] = p[:, C_MV:C_MO]
    mo_ref[0] = p[:, C_MO:C_AQ]
    av_ref[0] = p[:, C_AV:C_CU]
    cu_ref[0] = p[:, C_CU:C_LF]
    vt_ref[0] = pt[0:M_WIDTH]
    g = pt[M_WIDTH:NT_ROWS] + brow_ref[...]
    r = lax.broadcasted_iota(jnp.int32, g.shape, 0)
    gt_ref[0] = jnp.where((r // M_HEADS) % 2 == 1, _log_sigmoid(g), g)
    lfc_ref[0] = _log_sigmoid(p[:, C_LF:N_COLS] + bcol_ref[...])
    if rope:
        cos = cos_ref[...]
        sin = sin_ref[...]
        lane = lax.broadcasted_iota(jnp.int32, cos.shape, 1)
        first = (lane % 32) < 16

        def rot(z):
            sw = jnp.where(first, pltpu.roll(z, LANES - 16, 1), pltpu.roll(z, 16, 1))
            return z * cos + sw * sin

        for j in range(A_WIDTH // LANES):
            aq_ref[0, :, j * LANES:(j + 1) * LANES] = rot(p[:, C_AQ + j * LANES:C_AQ + (j + 1) * LANES])
        ak_ref[0] = rot(p[:, C_AK:C_AV])
    else:
        aq_ref[0] = p[:, C_AQ:C_AK]
        ak_ref[0] = p[:, C_AK:C_AV]


def _inproj(x, gain, shift, scale, w, wt, bcol, brow, cos, sin, *, rope, tm):
    b, s, _ = x.shape
    tok = lambda width: pl.BlockSpec((1, tm, width), lambda bi, i: (bi, i, 0))
    vec = pl.BlockSpec((1, 1, D_MODEL), lambda bi, i: (bi, 0, 0))
    tab = pl.BlockSpec((tm, LANES), lambda bi, i: (i, 0))
    tr = lambda rows: pl.BlockSpec((1, rows, tm), lambda bi, i: (bi, 0, i))
    widths = (M_WIDTH, M_WIDTH, M_WIDTH, M_WIDTH)
    out_specs = ([tok(wd) for wd in widths] + [tr(M_WIDTH), tr(N_GATES), tok(LANES),
                 tok(A_WIDTH), tok(KV_WIDTH), tok(KV_WIDTH), tok(2 * C_WIDTH)])
    out_shape = ([jax.ShapeDtypeStruct((b, s, wd), F32) for wd in widths]
                 + [jax.ShapeDtypeStruct((b, M_WIDTH, s), F32), jax.ShapeDtypeStruct((b, N_GATES, s), F32),
                    jax.ShapeDtypeStruct((b, s, LANES), F32), jax.ShapeDtypeStruct((b, s, A_WIDTH), F32),
                    jax.ShapeDtypeStruct((b, s, KV_WIDTH), F32), jax.ShapeDtypeStruct((b, s, KV_WIDTH), F32),
                    jax.ShapeDtypeStruct((b, s, 2 * C_WIDTH), F32)])
    return pl.pallas_call(
        functools.partial(_inproj_kernel, rope=rope),
        grid=(b, s // tm),
        in_specs=[tok(D_MODEL), _const_spec((1, D_MODEL)), vec, vec,
                  _const_spec((D_MODEL, N_COLS)), _const_spec((NT_ROWS, D_MODEL)),
                  _const_spec((1, LANES)), _const_spec((N_GATES, 1)), tab, tab],
        out_specs=out_specs,
        out_shape=out_shape,
        compiler_params=_cparams(("parallel", "parallel")),
        name="inproj_rope" if rope else "inproj_ctx",
    )(x, gain, shift, scale, w, wt, bcol, brow, cos, sin)


def _mlstm_kernel(qf_ref, kf_ref, vf_ref, vtf_ref, gtf_ref, lcf_ref,
                  qb_ref, kb_ref, vb_ref, vtb_ref, gtb_ref, lcb_ref, c0_ref, m0_ref,
                  hf_ref, hb_ref, cst_ref, mst_ref, *, t):
    @pl.when(pl.program_id(1) == 0)
    def _():
        cst_ref[...] = c0_ref[...]
        mst_ref[...] = m0_ref[...]

    row = lax.broadcasted_iota(jnp.int32, (t, t), 0)
    col = lax.broadcasted_iota(jnp.int32, (t, t), 1)
    dirs = ((qf_ref, kf_ref, vf_ref, vtf_ref, gtf_ref, lcf_ref, hf_ref),
            (qb_ref, kb_ref, vb_ref, vtb_ref, gtb_ref, lcb_ref, hb_ref))
    for d, (q_ref, k_ref, v_ref, vt_ref, gt_ref, lc_ref, h_ref) in enumerate(dirs):
        rev = d == 1
        mask = (col >= row) if rev else (col <= row)
        ones_ts = jnp.where(mask, 1.0, 0.0).astype(BF16)
        ones_st = jnp.where((row >= col) if rev else (row <= col), 1.0, 0.0).astype(BF16)
        g8 = gt_ref[0, d * 2 * M_HEADS:(d + 1) * 2 * M_HEADS, :]
        a8 = sum(_dot(part, ones_st) for part in _split_bf16(g8, 3))
        li_row = g8[0:M_HEADS]
        a_row = a8[M_HEADS:2 * M_HEADS]
        b_row = li_row - a_row
        a_end = a_row[:, 0:1] if rev else a_row[:, t - 1:t]
        a_col = sum(_dot(ones_ts, part) for part in _split_bf16(lc_ref[0], 3))
        q_all = q_ref[0]
        k_all = k_ref[0]
        v_all = v_ref[0]
        for hd in range(M_HEADS):
            ch = d * M_HEADS + hd
            sl = slice(hd * HEAD_DIM, (hd + 1) * HEAD_DIM)
            qh = q_all[:, sl].astype(BF16)
            kh = k_all[:, sl].astype(BF16)
            vh = v_all[:, sl].astype(BF16)
            s = _dot_nt(qh, kh)
            bh = b_row[hd:hd + 1]
            mb = jnp.where(mask, bh, -jnp.inf)
            cm = jnp.max(mb, axis=1, keepdims=True)
            m_prev = mst_ref[0, ch, 0:1, 0:1]
            mx = jnp.maximum(cm, m_prev)
            sp = s * jnp.exp(mb - mx)
            den_i = jnp.sum(sp, axis=1, keepdims=True)
            num_i = _dot(sp.astype(BF16), vh)
            cst = cst_ref[0, ch]
            qc = _dot_nt(qh, cst.astype(BF16))
            w = jnp.exp(m_prev - mx)
            num = w * qc[:, 0:HEAD_DIM] + num_i
            den = w * qc[:, HEAD_DIM:HEAD_DIM + 1] + den_i
            m_t = a_col[:, ch:ch + 1] + mx
            h_ref[0, :, sl] = num / jnp.maximum(jnp.abs(den), jnp.exp(-m_t))
            ae = a_end[hd:hd + 1]
            g = ae + bh
            m_new = jnp.maximum(ae + m_prev, jnp.max(g, axis=1, keepdims=True))
            decay = jnp.exp(ae + m_prev - m_new)
            wg = jnp.exp(g - m_new)
            vte = jnp.concatenate([vt_ref[0, sl, :], jnp.ones((STATE_ROWS - HEAD_DIM, t), F32)], axis=0)
            cst_ref[0, ch] = decay * cst + _dot((vte * wg).astype(BF16), kh)
            mst_ref[0, ch] = jnp.broadcast_to(m_new, (8, LANES))


def _mlstm(mq, mk, mv, vt, gt, lfc, c0, m0, *, t):
    b, s, _ = mq.shape
    nc = s // t
    fwd = lambda width: pl.BlockSpec((1, t, width), lambda bi, c: (bi, c, 0))
    bwd = lambda width: pl.BlockSpec((1, t, width), lambda bi, c: (bi, nc - 1 - c, 0))
    fwd_t = lambda rows: pl.BlockSpec((1, rows, t), lambda bi, c: (bi, 0, c))
    bwd_t = lambda rows: pl.BlockSpec((1, rows, t), lambda bi, c: (bi, 0, nc - 1 - c))
    st_c = pl.BlockSpec((1, N_CHAINS, STATE_ROWS, HEAD_DIM), lambda bi, c: (bi, 0, 0, 0))
    st_m = pl.BlockSpec((1, N_CHAINS, 8, LANES), lambda bi, c: (bi, 0, 0, 0))
    return pl.pallas_call(
        functools.partial(_mlstm_kernel, t=t),
        grid=(b, nc),
        in_specs=[fwd(M_WIDTH), fwd(M_WIDTH), fwd(M_WIDTH), fwd_t(M_WIDTH), fwd_t(N_GATES), fwd(LANES),
                  bwd(M_WIDTH), bwd(M_WIDTH), bwd(M_WIDTH), bwd_t(M_WIDTH), bwd_t(N_GATES), bwd(LANES),
                  st_c, st_m],
        out_specs=[fwd(M_WIDTH), bwd(M_WIDTH), st_c, st_m],
        out_shape=[jax.ShapeDtypeStruct((b, s, M_WIDTH), F32), jax.ShapeDtypeStruct((b, s, M_WIDTH), F32),
                   jax.ShapeDtypeStruct(c0.shape, F32), jax.ShapeDtypeStruct(m0.shape, F32)],
        compiler_params=_cparams(("parallel", "arbitrary")),
        name="mlstm_scan",
    )(mq, mk, mv, vt, gt, lfc, mq, mk, mv, vt, gt, lfc, c0, m0)


def _attn_kernel(*refs, seq, tq, local):
    if local:
        q_ref, kp_ref, kc_ref, kn_ref, vp_ref, vc_ref, vn_ref, kx_ref, vx_ref, sink_ref, o_ref = refs
    else:
        q_ref, kx_ref, vx_ref, sink_ref, o_ref = refs
    n_ctx = kx_ref.shape[1]
    q = q_ref[0]
    if local:
        k_all = jnp.concatenate([kp_ref[0], kc_ref[0], kn_ref[0], kx_ref[0]], axis=0)
        v_all = jnp.concatenate([vp_ref[0], vc_ref[0], vn_ref[0], vx_ref[0]], axis=0)
        n_loc = 3 * tq
        start = pl.program_id(1) * tq
        qpos = start + lax.broadcasted_iota(jnp.int32, (tq, n_loc), 0)
        kpos = start - tq + lax.broadcasted_iota(jnp.int32, (tq, n_loc), 1)
        ok = (jnp.abs(qpos - kpos) <= WINDOW) & (kpos >= 0) & (kpos < seq)
        bias = jnp.where(ok, 0.0, NEG_INF)
        bias = jnp.concatenate([bias] * A_GROUP, axis=0)
    else:
        k_all = kx_ref[0]
        v_all = vx_ref[0]
        n_loc = 0
    for g in range(A_KV_HEADS):
        qg = jnp.concatenate(
            [q[:, (g * A_GROUP + j) * HEAD_DIM:(g * A_GROUP + j + 1) * HEAD_DIM] for j in range(A_GROUP)],
            axis=0).astype(BF16)
        kg = k_all[:, g * HEAD_DIM:(g + 1) * HEAD_DIM].astype(BF16)
        vg = v_all[:, g * HEAD_DIM:(g + 1) * HEAD_DIM].astype(BF16)
        s = _dot_nt(qg, kg)
        if local:
            s = jnp.concatenate([jnp.where(bias < 0.0, NEG_INF, s[:, :n_loc]), s[:, n_loc:]], axis=1)
        sink = jnp.concatenate(
            [jnp.broadcast_to(sink_ref[0:1, g * A_GROUP + j:g * A_GROUP + j + 1], (tq, 1))
             for j in range(A_GROUP)], axis=0)
        m = jnp.maximum(jnp.max(s, axis=1, keepdims=True), sink)
        p = jnp.exp(s - m)
        l = jnp.sum(p, axis=1, keepdims=True) + jnp.exp(sink - m)
        o = _dot(p.astype(BF16), vg) / l
        for j in range(A_GROUP):
            hq = g * A_GROUP + j
            o_ref[0, :, hq * HEAD_DIM:(hq + 1) * HEAD_DIM] = o[j * tq:(j + 1) * tq]


def _attention(aq, ak, av, kx, vx, sink, *, local):
    b, s, _ = aq.shape
    n_ctx = kx.shape[1]
    tq = WINDOW
    nb = s // tq
    qs = pl.BlockSpec((1, tq, A_WIDTH), lambda bi, i: (bi, i, 0))
    cx = pl.BlockSpec((1, n_ctx, KV_WIDTH), lambda bi, i: (bi, 0, 0))
    sk = pl.BlockSpec((1, A_HEADS), lambda bi, i: (0, 0))
    if local:
        prv = pl.BlockSpec((1, tq, KV_WIDTH), lambda bi, i: (bi, jnp.maximum(i - 1, 0), 0))
        cur = pl.BlockSpec((1, tq, KV_WIDTH), lambda bi, i: (bi, i, 0))
        nxt = pl.BlockSpec((1, tq, KV_WIDTH), lambda bi, i: (bi, jnp.minimum(i + 1, nb - 1), 0))
        in_specs = [qs, prv, cur, nxt, prv, cur, nxt, cx, cx, sk]
        args = (aq, ak, ak, ak, av, av, av, kx, vx, sink)
    else:
        in_specs = [qs, cx, cx, sk]
        args = (aq, kx, vx, sink)
    return pl.pallas_call(
        functools.partial(_attn_kernel, seq=s, tq=tq, local=local),
        grid=(b, nb),
        in_specs=in_specs,
        out_specs=qs,
        out_shape=jax.ShapeDtypeStruct((b, s, A_WIDTH), F32),
        compiler_params=_cparams(("parallel", "parallel")),
        name="window_attn" if local else "ctx_attn",
    )(*args)


def _conv_kernel(up_ref, uc_ref, un_ref, dw_ref, db_ref, lg_ref, lb_ref, pw_ref, o_ref, ybuf, *, tc, sub):
    i = pl.program_id(1)
    nblk = pl.num_programs(1)

    def glu(u):
        return u[:, :C_WIDTH] * _sigmoid(u[:, C_WIDTH:])

    ybuf[0:CONV_HALO] = jnp.where(i > 0, glu(up_ref[0]), 0.0)
    ybuf[CONV_HALO:CONV_HALO + tc] = glu(uc_ref[0])
    ybuf[CONV_HALO + tc:2 * CONV_HALO + tc] = jnp.where(i < nblk - 1, glu(un_ref[0]), 0.0)
    off = CONV_HALO - CONV_K // 2
    for r0 in range(0, tc, sub):
        acc = jnp.zeros((sub, C_WIDTH), F32)
        for k in range(CONV_K):
            acc = acc + ybuf[r0 + off + k:r0 + off + k + sub, :] * dw_ref[k:k + 1, :]
        y = acc + db_ref[...]
        mu = jnp.mean(y, axis=-1, keepdims=True)
        yc = y - mu
        var = jnp.mean(yc * yc, axis=-1, keepdims=True)
        z = yc * lax.rsqrt(var + EPS) * lg_ref[...] + lb_ref[...]
        z = z * _sigmoid(z)
        o_ref[0, r0:r0 + sub, :] = _dot(z.astype(BF16), pw_ref[...])


def _conv(cu, dw, db, lg, lb, pw, *, tc):
    b, s, _ = cu.shape
    nh = s // CONV_HALO
    r = tc // CONV_HALO
    cur = pl.BlockSpec((1, tc, 2 * C_WIDTH), lambda bi, i: (bi, i, 0))
    prv = pl.BlockSpec((1, CONV_HALO, 2 * C_WIDTH), lambda bi, i: (bi, jnp.maximum(i * r - 1, 0), 0))
    nxt = pl.BlockSpec((1, CONV_HALO, 2 * C_WIDTH), lambda bi, i: (bi, jnp.minimum((i + 1) * r, nh - 1), 0))
    return pl.pallas_call(
        functools.partial(_conv_kernel, tc=tc, sub=64),
        grid=(b, s // tc),
        in_specs=[prv, cur, nxt, _const_spec((CONV_K + 1, C_WIDTH)), _const_spec((1, C_WIDTH)),
                  _const_spec((1, C_WIDTH)), _const_spec((1, C_WIDTH)), _const_spec((C_WIDTH, C_WIDTH))],
        out_specs=pl.BlockSpec((1, tc, C_WIDTH), lambda bi, i: (bi, i, 0)),
        out_shape=jax.ShapeDtypeStruct((b, s, C_WIDTH), F32),
        scratch_shapes=[pltpu.VMEM((tc + 2 * CONV_HALO, C_WIDTH), F32)],
        compiler_params=_cparams(("parallel", "parallel")),
        name="conformer_conv",
    )(cu, cu, cu, dw, db, lg, lb, pw)


def _mix_ffn_kernel(*refs, final, n_chunk):
    (x_ref, hf_ref, hb_ref, mo_ref, a_ref, c_ref, hg_ref, wo_ref, g1_ref, sh2_ref, sc2_ref, g2_ref,
     gain2_ref, wi_ref, wf_ref) = refs[:15]
    fg_ref = refs[15] if final else None
    o_ref = refs[-1]
    hs = hf_ref[0] + hb_ref[0]
    ri = lax.broadcasted_iota(jnp.int32, (M_WIDTH, M_WIDTH), 0) // HEAD_DIM
    ci = lax.broadcasted_iota(jnp.int32, (M_WIDTH, M_WIDTH), 1) // HEAD_DIM
    same_head = jnp.where(ri == ci, 1.0, 0.0).astype(BF16)
    ms = sum(_dot(part, same_head) for part in _split_bf16(hs * hs, 2)) * (1.0 / HEAD_DIM)
    mx = hs * lax.rsqrt(ms + EPS) * hg_ref[...] * _sigmoid(mo_ref[0])
    y = (_dot(mx.astype(BF16), wo_ref[0:M_WIDTH, :])
         + _dot(a_ref[0].astype(BF16), wo_ref[M_WIDTH:M_WIDTH + A_WIDTH, :])
         + _dot(c_ref[0].astype(BF16), wo_ref[M_WIDTH + A_WIDTH:, :]))
    x1 = x_ref[0] + g1_ref[0] * y
    h2 = (_rms(x1) * gain2_ref[...]) * (1.0 + sc2_ref[0]) + sh2_ref[0]
    hb2 = h2.astype(BF16)
    ch = FFN_HIDDEN // n_chunk
    acc = None
    for j in range(n_chunk):
        gate = _dot(hb2, wi_ref[:, j * ch:(j + 1) * ch])
        up = _dot(hb2, wi_ref[:, FFN_HIDDEN + j * ch:FFN_HIDDEN + (j + 1) * ch])
        hj = (gate * _sigmoid(gate) * up).astype(BF16)
        part = _dot(hj, wf_ref[j * ch:(j + 1) * ch, :])
        acc = part if acc is None else acc + part
    x2 = x1 + g2_ref[0] * acc
    if final:
        x2 = _rms(x2) * fg_ref[...]
    o_ref[0] = x2


def _mix_ffn(x, hf, hb, mo, a, c, hg, wo, g1, sh2, sc2, g2, gain2, wi, wf, fg, *, tm):
    b, s, _ = x.shape
    final = fg is not None
    tok = lambda width: pl.BlockSpec((1, tm, width), lambda bi, i: (bi, i, 0))
    vec = pl.BlockSpec((1, 1, D_MODEL), lambda bi, i: (bi, 0, 0))
    in_specs = [tok(D_MODEL), tok(M_WIDTH), tok(M_WIDTH), tok(M_WIDTH), tok(A_WIDTH), tok(C_WIDTH),
                _const_spec((1, M_WIDTH)), _const_spec((D_MODEL, D_MODEL)), vec, vec, vec, vec,
                _const_spec((1, D_MODEL)), _const_spec((D_MODEL, 2 * FFN_HIDDEN)),
                _const_spec((FFN_HIDDEN, D_MODEL))]
    args = [x, hf, hb, mo, a, c, hg, wo, g1, sh2, sc2, g2, gain2, wi, wf]
    if final:
        in_specs.append(_const_spec((1, D_MODEL)))
        args.append(fg)
    return pl.pallas_call(
        functools.partial(_mix_ffn_kernel, final=final, n_chunk=2),
        grid=(b, s // tm),
        in_specs=in_specs,
        out_specs=tok(D_MODEL),
        out_shape=jax.ShapeDtypeStruct((b, s, D_MODEL), F32),
        compiler_params=_cparams(("parallel", "parallel")),
        name="mix_ffn_final" if final else "mix_ffn",
    )(*args)


def _rope_tables(seq):
    half = HEAD_DIM // 4
    freqs = ROPE_THETA ** (-jnp.arange(half, dtype=F32) / half)
    t = jnp.arange(seq)
    sign = jnp.concatenate([-jnp.ones((half,), F32), jnp.ones((half,), F32)])
    cos_parts, sin_parts = [], []
    for pos in (t // GRID_W, t % GRID_W):
        ang = pos.astype(F32)[:, None] * freqs[None, :]
        c, s = jnp.cos(ang), jnp.sin(ang)
        cos_parts.append(jnp.concatenate([c, c], axis=-1))
        sin_parts.append(jnp.concatenate([s, s], axis=-1) * sign)
    cos = jnp.concatenate(cos_parts, axis=-1)
    sin = jnp.concatenate(sin_parts, axis=-1)
    rep = LANES // HEAD_DIM
    return jnp.tile(cos, (1, rep)), jnp.tile(sin, (1, rep))


def _layer_weights(w_in, gate_bias):
    o = np.cumsum((0,) + IN_SPLITS)
    col = lambda i: w_in[:, int(o[i]):int(o[i + 1])]
    scale = HEAD_DIM ** -0.5
    gates = col(4)
    f_cols = jnp.concatenate([gates[:, d * 2 * M_HEADS + M_HEADS:(d + 1) * 2 * M_HEADS] for d in range(N_DIR)], axis=1)
    f_cols = jnp.pad(f_cols, ((0, 0), (0, LANES - N_CHAINS)))
    w = jnp.concatenate([col(0), col(1) * scale, col(2), col(3), col(5) * scale, col(6), col(7), col(8), f_cols],
                        axis=1).astype(BF16)
    wt = jnp.concatenate([col(2).T, gates.T], axis=0).astype(BF16)
    brow = gate_bias.reshape(N_GATES, 1).astype(F32)
    bcol = jnp.pad(gate_bias[:, 1, :].reshape(1, N_CHAINS).astype(F32), ((0, 0), (0, LANES - N_CHAINS)))
    return w, wt, bcol, brow


def kernel(x, c, ctx, c_ctx, w_mod, b_mod, norm_gain, w_in, mlstm_gate_bias, mlstm_head_gain, attn_sink,
           conv_dw_w, conv_dw_b, conv_ln_g, conv_ln_b, conv_pw_w, w_out, w_ffn_in, w_ffn_out, final_gain):
    b, seq, _ = x.shape
    n_ctx = ctx.shape[1]
    rows = -(-(b + 1) // 8) * 8
    cc = jnp.zeros((rows, D_MODEL), F32).at[:b].set(c).at[b].set(c_ctx)
    mod = _modulation(cc, w_mod, b_mod)
    cos_x, sin_x = _rope_tables(seq)
    cos_c, sin_c = cos_x[:n_ctx], sin_x[:n_ctx]
    zero_c = jnp.zeros((b, N_CHAINS, STATE_ROWS, HEAD_DIM), F32)
    zero_m = jnp.zeros((b, N_CHAINS, 8, LANES), F32)
    tm_x = min(512, seq)
    tm_c = min(256, n_ctx)
    t_x = min(256, seq)
    t_c = min(256, n_ctx)

    for l in range(DEPTH):
        last = l == DEPTH - 1
        mx_ = mod[l, :b].reshape(b, 1, 6 * D_MODEL)
        mc_ = jnp.broadcast_to(mod[l, b].reshape(1, 1, 6 * D_MODEL), (b, 1, 6 * D_MODEL))
        part = lambda m, i: m[:, :, i * D_MODEL:(i + 1) * D_MODEL]
        w, wt, bcol, brow = _layer_weights(w_in[l], mlstm_gate_bias[l])
        gain1 = norm_gain[l, 0].reshape(1, D_MODEL)
        gain2 = norm_gain[l, 1].reshape(1, D_MODEL)
        hg = mlstm_head_gain[l].reshape(1, M_WIDTH)
        sink = attn_sink[l].reshape(1, A_HEADS).astype(F32)
        dw = jnp.pad(conv_dw_w[l], ((0, 1), (0, 0)))
        db = conv_dw_b[l].reshape(1, C_WIDTH)
        lg = conv_ln_g[l].reshape(1, C_WIDTH)
        lb = conv_ln_b[l].reshape(1, C_WIDTH)
        pw = conv_pw_w[l].astype(BF16)
        wo = w_out[l].astype(BF16)
        wi = w_ffn_in[l].astype(BF16)
        wf = w_ffn_out[l].astype(BF16)

        px = _inproj(x, gain1, part(mx_, 0), part(mx_, 1), w, wt, bcol, brow, cos_x, sin_x, rope=True, tm=tm_x)
        pc = _inproj(ctx, gain1, part(mc_, 0), part(mc_, 1), w, wt, bcol, brow, cos_c, sin_c, rope=False, tm=tm_c)
        mq, mk, mv, mo, vt, gt, lfc, aq, ak, av, cu = px
        cmq, cmk, cmv, cmo, cvt, cgt, clfc, caq, cak, cav, ccu = pc

        chf, chb, cst, mst = _mlstm(cmq, cmk, cmv, cvt, cgt, clfc, zero_c, zero_m, t=t_c)
        hf, hb, _, _ = _mlstm(mq, mk, mv, vt, gt, lfc, cst, mst, t=t_x)
        a_x = _attention(aq, ak, av, cak, cav, sink, local=True)
        c_x = _conv(cu, dw, db, lg, lb, pw, tc=tm_x)
        x = _mix_ffn(x, hf, hb, mo, a_x, c_x, hg, wo, part(mx_, 2), part(mx_, 3), part(mx_, 4), part(mx_, 5),
                     gain2, wi, wf, final_gain.reshape(1, D_MODEL) if last else None, tm=tm_x)
        if not last:
            a_c = _attention(caq, cak, cav, cak, cav, sink, local=False)
            c_c = _conv(ccu, dw, db, lg, lb, pw, tc=tm_c)
            ctx = _mix_ffn(ctx, chf, chb, cmo, a_c, c_c, hg, wo, part(mc_, 2), part(mc_, 3), part(mc_, 4),
                           part(mc_, 5), gain2, wi, wf, None, tm=tm_c)
    return x
```

```python
import functools

import numpy as np
import jax
import jax.numpy as jnp
from jax import lax
from jax.experimental import pallas as pl
from jax.experimental.pallas import tpu as pltpu

F32 = jnp.float32
BF16 = jnp.bfloat16

D_MODEL = 1024
DEPTH = 2
GRID_W = 64
HEAD_DIM = 64
M_WIDTH = D_MODEL // 4
M_HEADS = M_WIDTH // HEAD_DIM
N_DIR = 2
A_WIDTH = D_MODEL // 2
A_HEADS = A_WIDTH // HEAD_DIM
A_KV_HEADS = A_HEADS // 4
A_GROUP = A_HEADS // A_KV_HEADS
KV_WIDTH = A_KV_HEADS * HEAD_DIM
WINDOW = 128
ROPE_THETA = 10000.0
C_WIDTH = D_MODEL // 4
CONV_K = 31
CONV_HALO = 16
FFN_HIDDEN = ((8 * D_MODEL + 3 * 256 - 1) // (3 * 256)) * 256
IN_SPLITS = (M_WIDTH, M_WIDTH, M_WIDTH, M_WIDTH, N_DIR * 2 * M_HEADS,
             A_WIDTH, KV_WIDTH, KV_WIDTH, 2 * C_WIDTH)
EPS = 1e-6
NEG_INF = -1e30

LANES = 128
SUBLANES = 8
VMEM_LIMIT = 52 * 1024 * 1024

C_MQ = 0
C_MK = C_MQ + M_WIDTH
C_MO = C_MK + M_WIDTH
C_AQ = C_MO + M_WIDTH
C_AK = C_AQ + A_WIDTH
C_AV = C_AK + KV_WIDTH
C_CU = C_AV + KV_WIDTH
N_COLS = C_CU + 2 * C_WIDTH
GATE_ROWS = N_DIR * SUBLANES
R_LI = M_WIDTH
R_LF = R_LI + GATE_ROWS
NT_ROWS = R_LF + GATE_ROWS
M_PAIRS = M_HEADS // 2
PAIR_W = 2 * HEAD_DIM
STATE_ROWS = PAIR_W + SUBLANES
M_CHUNK = LANES


def _cparams(sem):
    return pltpu.CompilerParams(dimension_semantics=sem, vmem_limit_bytes=VMEM_LIMIT)


def _const_spec(shape):
    nd = len(shape)
    return pl.BlockSpec(shape, lambda *_: (0,) * nd, pipeline_mode=pl.Buffered(1))


def _log_sigmoid(x):
    return jnp.minimum(x, 0.0) - jnp.log1p(jnp.exp(-jnp.abs(x)))


def _sigmoid(x):
    return jax.nn.sigmoid(x)


def _rms(x):
    return x * lax.rsqrt(jnp.mean(x * x, axis=-1, keepdims=True) + EPS)


def _dot(a, b):
    return jnp.dot(a, b, preferred_element_type=F32)


def _dot_nt(a, b):
    return lax.dot_general(a, b, (((1,), (1,)), ((), ())), preferred_element_type=F32)


def _split_bf16(x, parts):
    out = []
    r = x
    for _ in range(parts):
        p = r.astype(BF16)
        out.append(p)
        r = r - p.astype(F32)
    return out


def _rows(x, i, n):
    return jnp.broadcast_to(x[i:i + 1], (n, x.shape[1]))


def _mod_kernel(c_ref, w_ref, b_ref, o_ref):
    cc = c_ref[...]
    s = (cc * _sigmoid(cc)).astype(BF16)
    o_ref[0] = _dot(s, w_ref[0].astype(BF16)) + b_ref[0]


def _modulation(cc, w_mod, b_mod):
    rows = cc.shape[0]
    n = w_mod.shape[-1]
    tn = 1536
    return pl.pallas_call(
        _mod_kernel,
        grid=(DEPTH, n // tn),
        in_specs=[pl.BlockSpec((rows, D_MODEL), lambda l, j: (0, 0)),
                  pl.BlockSpec((1, D_MODEL, tn), lambda l, j: (l, 0, j)),
                  pl.BlockSpec((1, 1, tn), lambda l, j: (l, 0, j))],
        out_specs=pl.BlockSpec((1, rows, tn), lambda l, j: (l, 0, j)),
        out_shape=jax.ShapeDtypeStruct((DEPTH, rows, n), F32),
        compiler_params=_cparams(("arbitrary", "arbitrary")),
        name="modulation",
    )(cc, w_mod, b_mod.reshape(DEPTH, 1, n))


def _inproj_kernel(x_ref, gain_ref, shift_ref, scale_ref, w_ref, wt_ref, brow_ref, cos_ref, sin_ref,
                   mq_ref, mk_ref, mo_ref, vt_ref, li_ref, lf_ref, aq_ref, ak_ref, av_ref, cu_ref, *, rope):
    xf = x_ref[0]
    h = _rms(xf) * gain_ref[...]
    h = h * (1.0 + scale_ref[0]) + shift_ref[0]
    hb = h.astype(BF16)
    p = _dot(hb, w_ref[...])
    pt = _dot_nt(wt_ref[...], hb)
    mq_ref[0] = p[:, C_MQ:C_MK]
    mk_ref[0] = p[:, C_MK:C_MO]
    mo_ref[0] = p[:, C_MO:C_AQ]
    av_ref[0] = p[:, C_AV:C_CU]
    cu_ref[0] = p[:, C_CU:N_COLS]
    vt_ref[0] = pt[0:M_WIDTH]
    g = pt[R_LI:NT_ROWS] + brow_ref[...]
    li_ref[0] = g[0:GATE_ROWS]
    lf = g[GATE_ROWS:2 * GATE_ROWS]
    r = lax.broadcasted_iota(jnp.int32, lf.shape, 0)
    lf_ref[0] = jnp.where(r % SUBLANES < M_HEADS, _log_sigmoid(lf), 0.0)
    if rope:
        cos = cos_ref[...]
        sin = sin_ref[...]
        lane = lax.broadcasted_iota(jnp.int32, cos.shape, 1)
        first = (lane % 32) < 16

        def rot(z):
            sw = jnp.where(first, pltpu.roll(z, LANES - 16, 1), pltpu.roll(z, 16, 1))
            return z * cos + sw * sin

        for j in range(A_WIDTH // LANES):
            aq_ref[0, :, j * LANES:(j + 1) * LANES] = rot(p[:, C_AQ + j * LANES:C_AQ + (j + 1) * LANES])
        ak_ref[0] = rot(p[:, C_AK:C_AV])
    else:
        aq_ref[0] = p[:, C_AQ:C_AK]
        ak_ref[0] = p[:, C_AK:C_AV]


def _inproj(x, gain, shift, scale, w, wt, brow, cos, sin, *, rope, tm):
    b, s, _ = x.shape
    tok = lambda width: pl.BlockSpec((1, tm, width), lambda bi, i: (bi, i, 0))
    vec = pl.BlockSpec((1, 1, D_MODEL), lambda bi, i: (bi, 0, 0))
    tab = pl.BlockSpec((tm, LANES), lambda bi, i: (i, 0))
    tr = lambda rows: pl.BlockSpec((1, rows, tm), lambda bi, i: (bi, 0, i))
    tok_shape = lambda width: jax.ShapeDtypeStruct((b, s, width), F32)
    tr_shape = lambda rows: jax.ShapeDtypeStruct((b, rows, s), F32)
    out_specs = [tok(M_WIDTH), tok(M_WIDTH), tok(M_WIDTH), tr(M_WIDTH), tr(GATE_ROWS), tr(GATE_ROWS),
                 tok(A_WIDTH), tok(KV_WIDTH), tok(KV_WIDTH), tok(2 * C_WIDTH)]
    out_shape = [tok_shape(M_WIDTH), tok_shape(M_WIDTH), tok_shape(M_WIDTH), tr_shape(M_WIDTH),
                 tr_shape(GATE_ROWS), tr_shape(GATE_ROWS),
                 tok_shape(A_WIDTH), tok_shape(KV_WIDTH), tok_shape(KV_WIDTH), tok_shape(2 * C_WIDTH)]
    return pl.pallas_call(
        functools.partial(_inproj_kernel, rope=rope),
        grid=(b, s // tm),
        in_specs=[tok(D_MODEL), _const_spec((1, D_MODEL)), vec, vec,
                  _const_spec((D_MODEL, N_COLS)), _const_spec((NT_ROWS, D_MODEL)),
                  _const_spec((2 * GATE_ROWS, 1)), tab, tab],
        out_specs=out_specs,
        out_shape=out_shape,
        compiler_params=_cparams(("parallel", "parallel")),
        name="inproj_rope" if rope else "inproj_ctx",
    )(x, gain, shift, scale, w, wt, brow, cos, sin)


def _mlstm_kernel(qf_ref, kf_ref, vtf_ref, lif_ref, lff_ref, qb_ref, kb_ref, vtb_ref, lib_ref, lfb_ref,
                  c0_ref, m0_ref, hf_ref, hb_ref, cst_ref, mst_ref, *, nsub):
    t = M_CHUNK

    @pl.when(pl.program_id(1) == 0)
    def _():
        cst_ref[...] = c0_ref[...]
        mst_ref[...] = m0_ref[...]

    key = lax.broadcasted_iota(jnp.int32, (t, t), 0)
    qry = lax.broadcasted_iota(jnp.int32, (t, t), 1)
    key2 = lax.broadcasted_iota(jnp.int32, (2 * t, t), 0) % t
    qry2 = lax.broadcasted_iota(jnp.int32, (2 * t, t), 1)
    lane8 = lax.broadcasted_iota(jnp.int32, (SUBLANES, t), 1)
    first_head_lanes = lax.broadcasted_iota(jnp.int32, (t, PAIR_W), 1) < HEAD_DIM
    first_head_rows = lax.broadcasted_iota(jnp.int32, (PAIR_W, t), 0) < HEAD_DIM
    den_r = lax.broadcasted_iota(jnp.int32, (SUBLANES, 2 * t), 0)
    den_c = lax.broadcasted_iota(jnp.int32, (SUBLANES, 2 * t), 1) // t
    st_r = lax.broadcasted_iota(jnp.int32, (STATE_ROWS, PAIR_W), 0)
    st_c = lax.broadcasted_iota(jnp.int32, (STATE_ROWS, PAIR_W), 1) // HEAD_DIM
    dirs = ((qf_ref, kf_ref, vtf_ref, lif_ref, lff_ref, hf_ref),
            (qb_ref, kb_ref, vtb_ref, lib_ref, lfb_ref, hb_ref))
    chains = [(d, p) for d in range(N_DIR) for p in range(M_PAIRS)]
    visible = [key <= qry, key >= qry]
    visible2 = [key2 <= qry2, key2 >= qry2]
    ones_st = [jnp.where(v, 1.0, 0.0).astype(BF16) for v in visible]
    state = {dp: cst_ref[0, dp[0] * M_PAIRS + dp[1]] for dp in chains}
    m_state = [mst_ref[0, d] for d in range(N_DIR)]

    gates, pairs = {}, {}
    for step in range(nsub):
        for d in range(N_DIR):
            rev = d == 1
            j = nsub - 1 - step if rev else step
            tok = slice(j * t, (j + 1) * t)
            gate_rows = slice(d * SUBLANES, (d + 1) * SUBLANES)
            li8 = dirs[d][3][0, gate_rows, tok]
            lf8 = dirs[d][4][0, gate_rows, tok]
            a8 = sum(_dot(part, ones_st[d]) for part in _split_bf16(lf8, 3))
            b8 = li8 - a8
            cm8 = b8
            sh = 1
            while sh < t:
                if rev:
                    moved = jnp.where(lane8 < t - sh, pltpu.roll(cm8, t - sh, 1), -jnp.inf)
                else:
                    moved = jnp.where(lane8 >= sh, pltpu.roll(cm8, sh, 1), -jnp.inf)
                cm8 = jnp.maximum(cm8, moved)
                sh *= 2
            a_end = jnp.broadcast_to(a8[:, 0:1] if rev else a8[:, t - 1:t], (SUBLANES, t))
            g8 = a_end + b8
            g_max = jnp.broadcast_to(jnp.max(g8, axis=1, keepdims=True), (SUBLANES, t))
            b_col = jnp.transpose(jnp.concatenate([b8, jnp.zeros((t - SUBLANES, t), F32)], axis=0))
            gates[step, d] = (a8, cm8, a_end, g8, g_max)
            for p in range(M_PAIRS):
                h0, h1 = 2 * p, 2 * p + 1
                lanes = slice(p * PAIR_W, (p + 1) * PAIR_W)
                q01 = dirs[d][0][0, tok, lanes].astype(BF16)
                k01 = dirs[d][1][0, tok, lanes]
                vt01 = dirs[d][2][0, lanes, tok]
                kbd = jnp.concatenate([jnp.where(first_head_lanes, k01, 0.0),
                                       jnp.where(first_head_lanes, 0.0, k01)], axis=0).astype(BF16)
                s2 = _dot_nt(kbd, q01)
                bc = jnp.concatenate([jnp.broadcast_to(b_col[:, h0:h0 + 1], (t, t)),
                                      jnp.broadcast_to(b_col[:, h1:h1 + 1], (t, t))], axis=0)
                vbd = jnp.concatenate([jnp.where(first_head_rows, vt01, 0.0),
                                       jnp.where(first_head_rows, 0.0, vt01)], axis=1)
                den_rows = jnp.where(((den_r == h0) & (den_c == 0)) | ((den_r == h1) & (den_c == 1)), 1.0, 0.0)
                lhs = jnp.concatenate([vbd, den_rows], axis=0).astype(BF16)
                pairs[step, d, p] = (tok, lanes, q01, k01.astype(BF16), vt01, s2, bc, lhs)

    for step in range(nsub):
        small = {}
        for d in range(N_DIR):
            a8, cm8, a_end, g8, g_max = gates[step, d]
            m_prev = m_state[d]
            mx8 = jnp.maximum(cm8, m_prev)
            w8 = jnp.exp(m_prev - mx8)
            e8 = jnp.exp(-(a8 + mx8))
            m_new = jnp.maximum(a_end + m_prev, g_max)
            decay8 = jnp.exp(a_end + m_prev - m_new)
            wg8 = jnp.exp(g8 - m_new)
            m_state[d] = m_new
            small[d] = (mx8, w8, e8, decay8, wg8)
        weighted = {}
        for d, p in chains:
            tok, lanes, q01, k01b, vt01, s2, bc, lhs = pairs[step, d, p]
            mx8 = small[d][0]
            mr = jnp.concatenate([_rows(mx8, 2 * p, t), _rows(mx8, 2 * p + 1, t)], axis=0)
            weighted[d, p] = (s2 * jnp.where(visible2[d], jnp.exp(bc - mr), 0.0)).astype(BF16)
        for d, p in chains:
            tok, lanes, q01, k01b, vt01, s2, bc, lhs = pairs[step, d, p]
            mx8, w8, e8, decay8, wg8 = small[d]
            h0, h1 = 2 * p, 2 * p + 1
            st = state[d, p]
            intra = _dot(lhs, weighted[d, p])
            inter = _dot_nt(st.astype(BF16), q01)
            w_rows = jnp.concatenate([_rows(w8, h0, HEAD_DIM), _rows(w8, h1, HEAD_DIM), w8], axis=0)
            tot = w_rows * inter + intra
            den = jnp.concatenate([_rows(tot, PAIR_W + h0, HEAD_DIM), _rows(tot, PAIR_W + h1, HEAD_DIM)], axis=0)
            e_rows = jnp.concatenate([_rows(e8, h0, HEAD_DIM), _rows(e8, h1, HEAD_DIM)], axis=0)
            h_t = tot[0:PAIR_W] / jnp.maximum(jnp.abs(den), e_rows)
            dirs[d][5][0, tok, lanes] = jnp.transpose(h_t)
            wg_rows = jnp.concatenate([_rows(wg8, h0, HEAD_DIM), _rows(wg8, h1, HEAD_DIM)], axis=0)
            upd = _dot(jnp.concatenate([vt01 * wg_rows, wg8], axis=0).astype(BF16), k01b)
            keep = (st_r // HEAD_DIM == st_c) | (st_r - PAIR_W == st_c + h0)
            dec = jnp.concatenate([_rows(decay8, h0, HEAD_DIM), _rows(decay8, h1, HEAD_DIM), decay8], axis=0)
            state[d, p] = dec * st + jnp.where(keep, upd, 0.0)

    for d, p in chains:
        cst_ref[0, d * M_PAIRS + p] = state[d, p]
    for d in range(N_DIR):
        mst_ref[0, d] = m_state[d]


def _mlstm(mq, mk, vt, li, lf, c0, m0, *, tb):
    b, s, _ = mq.shape
    nc = s // tb
    fwd = lambda width: pl.BlockSpec((1, tb, width), lambda bi, c: (bi, c, 0))
    bwd = lambda width: pl.BlockSpec((1, tb, width), lambda bi, c: (bi, nc - 1 - c, 0))
    fwd_t = lambda rows: pl.BlockSpec((1, rows, tb), lambda bi, c: (bi, 0, c))
    bwd_t = lambda rows: pl.BlockSpec((1, rows, tb), lambda bi, c: (bi, 0, nc - 1 - c))
    st_c = pl.BlockSpec((1, N_DIR * M_PAIRS, STATE_ROWS, PAIR_W), lambda bi, c: (bi, 0, 0, 0))
    st_m = pl.BlockSpec((1, N_DIR, SUBLANES, LANES), lambda bi, c: (bi, 0, 0, 0))
    return pl.pallas_call(
        functools.partial(_mlstm_kernel, nsub=tb // M_CHUNK),
        grid=(b, nc),
        in_specs=[fwd(M_WIDTH), fwd(M_WIDTH), fwd_t(M_WIDTH), fwd_t(GATE_ROWS), fwd_t(GATE_ROWS),
                  bwd(M_WIDTH), bwd(M_WIDTH), bwd_t(M_WIDTH), bwd_t(GATE_ROWS), bwd_t(GATE_ROWS),
                  st_c, st_m],
        out_specs=[fwd(M_WIDTH), bwd(M_WIDTH), st_c, st_m],
        out_shape=[jax.ShapeDtypeStruct((b, s, M_WIDTH), F32), jax.ShapeDtypeStruct((b, s, M_WIDTH), F32),
                   jax.ShapeDtypeStruct(c0.shape, F32), jax.ShapeDtypeStruct(m0.shape, F32)],
        compiler_params=_cparams(("parallel", "arbitrary")),
        name="mlstm_scan",
    )(mq, mk, vt, li, lf, mq, mk, vt, li, lf, c0, m0)


def _attn_kernel(*refs, seq, tq, local, per):
    if local:
        q_ref, kp_ref, kc_ref, kn_ref, vp_ref, vc_ref, vn_ref, kx_ref, vx_ref, sink_ref, o_ref = refs
    else:
        q_ref, kx_ref, vx_ref, sink_ref, o_ref = refs
    q = q_ref[0]
    if local:
        k_all = jnp.concatenate([kp_ref[0], kn_ref[0], kc_ref[0], kx_ref[0]], axis=0)
        v_all = jnp.concatenate([vp_ref[0], vn_ref[0], vc_ref[0], vx_ref[0]], axis=0)
        start = pl.program_id(1) * tq
        qpos = start + lax.broadcasted_iota(jnp.int32, (tq, 2 * tq), 0)
        side = lax.broadcasted_iota(jnp.int32, (tq, 2 * tq), 1)
        kpos = start + jnp.where(side < tq, side - tq, side)
        ok = (jnp.abs(qpos - kpos) <= WINDOW) & (kpos >= 0) & (kpos < seq)
        bias = jnp.where(ok, 0.0, NEG_INF)
        bias = jnp.concatenate([bias] * per, axis=0)
        n_edge = 2 * tq
    else:
        k_all = kx_ref[0]
        v_all = vx_ref[0]
    n_keys = k_all.shape[0]
    units = [(hq0 // A_GROUP, hq0) for hq0 in range(0, A_HEADS, per)]
    kb = [k_all[:, g * HEAD_DIM:(g + 1) * HEAD_DIM].astype(BF16) for g in range(A_KV_HEADS)]
    vb = [jnp.concatenate([v_all[:, g * HEAD_DIM:(g + 1) * HEAD_DIM], jnp.ones((n_keys, HEAD_DIM), F32)],
                          axis=1).astype(BF16) for g in range(A_KV_HEADS)]
    scores, sinks = [], []
    for g, hq0 in units:
        qg = jnp.concatenate([q[:, (hq0 + j) * HEAD_DIM:(hq0 + j + 1) * HEAD_DIM] for j in range(per)],
                             axis=0).astype(BF16)
        s = _dot_nt(qg, kb[g])
        if local:
            s = jnp.concatenate([s[:, :n_edge] + bias, s[:, n_edge:]], axis=1)
        scores.append(s)
        sinks.append(jnp.concatenate(
            [jnp.broadcast_to(sink_ref[0:1, hq0 + j:hq0 + j + 1], (tq, 1)) for j in range(per)], axis=0))
    maxes = [jnp.maximum(jnp.max(s, axis=1, keepdims=True), sk) for s, sk in zip(scores, sinks)]
    probs = [jnp.exp(s - m).astype(BF16) for s, m in zip(scores, maxes)]
    outs = [_dot(p, vb[g]) for p, (g, _) in zip(probs, units)]
    for ol, sk, m, (_, hq0) in zip(outs, sinks, maxes, units):
        o = ol[:, 0:HEAD_DIM] / (ol[:, HEAD_DIM:HEAD_DIM + 1] + jnp.exp(sk - m))
        for j in range(per):
            o_ref[0, :, (hq0 + j) * HEAD_DIM:(hq0 + j + 1) * HEAD_DIM] = o[j * tq:(j + 1) * tq]


def _attention(aq, ak, av, kx, vx, sink, *, local):
    b, s, _ = aq.shape
    n_ctx = kx.shape[1]
    tq = WINDOW
    nb = s // tq
    qs = pl.BlockSpec((1, tq, A_WIDTH), lambda bi, i: (bi, i, 0))
    cx = pl.BlockSpec((1, n_ctx, KV_WIDTH), lambda bi, i: (bi, 0, 0))
    sk = pl.BlockSpec((1, A_HEADS), lambda bi, i: (0, 0))
    if local:
        prv = pl.BlockSpec((1, tq, KV_WIDTH), lambda bi, i: (bi, jnp.maximum(i - 1, 0), 0))
        cur = pl.BlockSpec((1, tq, KV_WIDTH), lambda bi, i: (bi, i, 0))
        nxt = pl.BlockSpec((1, tq, KV_WIDTH), lambda bi, i: (bi, jnp.minimum(i + 1, nb - 1), 0))
        in_specs = [qs, prv, cur, nxt, prv, cur, nxt, cx, cx, sk]
        args = (aq, ak, ak, ak, av, av, av, kx, vx, sink)
    else:
        in_specs = [qs, cx, cx, sk]
        args = (aq, kx, vx, sink)
    return pl.pallas_call(
        functools.partial(_attn_kernel, seq=s, tq=tq, local=local, per=A_GROUP),
        grid=(b, nb),
        in_specs=in_specs,
        out_specs=qs,
        out_shape=jax.ShapeDtypeStruct((b, s, A_WIDTH), F32),
        compiler_params=_cparams(("parallel", "parallel")),
        name="window_attn" if local else "ctx_attn",
    )(*args)


def _conv_kernel(up_ref, uc_ref, un_ref, dw_ref, db_ref, lg_ref, lb_ref, pw_ref, o_ref, ybuf, *, tc, sub):
    i = pl.program_id(1)
    nblk = pl.num_programs(1)
    span = tc + 2 * CONV_HALO - SUBLANES

    def glu(u):
        return u[:, :C_WIDTH] * _sigmoid(u[:, C_WIDTH:])

    ybuf[0, 0:CONV_HALO] = jnp.where(i > 0, glu(up_ref[0]), 0.0)
    ybuf[0, CONV_HALO:CONV_HALO + tc] = glu(uc_ref[0])
    ybuf[0, CONV_HALO + tc:2 * CONV_HALO + tc] = jnp.where(i < nblk - 1, glu(un_ref[0]), 0.0)
    for r in range(1, SUBLANES):
        ybuf[r, 0:span] = ybuf[0, r:r + span]
    off = CONV_HALO - CONV_K // 2
    for r0 in range(0, tc, sub):
        acc = jnp.zeros((sub, C_WIDTH), F32)
        for k in range(CONV_K):
            r = (off + k) % SUBLANES
            base = r0 + off + k - r
            acc = acc + ybuf[r, base:base + sub, :] * dw_ref[k:k + 1, :]
        y = acc + db_ref[...]
        mu = jnp.mean(y, axis=-1, keepdims=True)
        yc = y - mu
        var = jnp.mean(yc * yc, axis=-1, keepdims=True)
        z = yc * lax.rsqrt(var + EPS) * lg_ref[...] + lb_ref[...]
        z = z * _sigmoid(z)
        o_ref[0, r0:r0 + sub, :] = _dot(z.astype(BF16), pw_ref[...])


def _conv(cu, dw, db, lg, lb, pw, *, tc):
    b, s, _ = cu.shape
    nh = s // CONV_HALO
    r = tc // CONV_HALO
    cur = pl.BlockSpec((1, tc, 2 * C_WIDTH), lambda bi, i: (bi, i, 0))
    prv = pl.BlockSpec((1, CONV_HALO, 2 * C_WIDTH), lambda bi, i: (bi, jnp.maximum(i * r - 1, 0), 0))
    nxt = pl.BlockSpec((1, CONV_HALO, 2 * C_WIDTH), lambda bi, i: (bi, jnp.minimum((i + 1) * r, nh - 1), 0))
    return pl.pallas_call(
        functools.partial(_conv_kernel, tc=tc, sub=64),
        grid=(b, s // tc),
        in_specs=[prv, cur, nxt, _const_spec((CONV_K + 1, C_WIDTH)), _const_spec((1, C_WIDTH)),
                  _const_spec((1, C_WIDTH)), _const_spec((1, C_WIDTH)), _const_spec((C_WIDTH, C_WIDTH))],
        out_specs=pl.BlockSpec((1, tc, C_WIDTH), lambda bi, i: (bi, i, 0)),
        out_shape=jax.ShapeDtypeStruct((b, s, C_WIDTH), F32),
        scratch_shapes=[pltpu.VMEM((SUBLANES, tc + 2 * CONV_HALO, C_WIDTH), F32)],
        compiler_params=_cparams(("parallel", "parallel")),
        name="conformer_conv",
    )(cu, cu, cu, dw, db, lg, lb, pw)


def _mix_ffn_kernel(*refs, final, n_chunk):
    (x_ref, hf_ref, hb_ref, mo_ref, a_ref, c_ref, hg_ref, wo_ref, g1_ref, sh2_ref, sc2_ref, g2_ref,
     gain2_ref, wi_ref, wf_ref) = refs[:15]
    fg_ref = refs[15] if final else None
    o_ref = refs[-1]
    hs = hf_ref[0] + hb_ref[0]
    ri = lax.broadcasted_iota(jnp.int32, (M_WIDTH, M_WIDTH), 0) // HEAD_DIM
    ci = lax.broadcasted_iota(jnp.int32, (M_WIDTH, M_WIDTH), 1) // HEAD_DIM
    same_head = jnp.where(ri == ci, 1.0, 0.0).astype(BF16)
    ms = sum(_dot(part, same_head) for part in _split_bf16(hs * hs, 2)) * (1.0 / HEAD_DIM)
    mx = hs * lax.rsqrt(ms + EPS) * hg_ref[...] * _sigmoid(mo_ref[0])
    y = (_dot(mx.astype(BF16), wo_ref[0:M_WIDTH, :])
         + _dot(a_ref[0].astype(BF16), wo_ref[M_WIDTH:M_WIDTH + A_WIDTH, :])
         + _dot(c_ref[0].astype(BF16), wo_ref[M_WIDTH + A_WIDTH:, :]))
    x1 = x_ref[0] + g1_ref[0] * y
    h2 = (_rms(x1) * gain2_ref[...]) * (1.0 + sc2_ref[0]) + sh2_ref[0]
    hb2 = h2.astype(BF16)
    ch = FFN_HIDDEN // n_chunk
    acc = None
    for j in range(n_chunk):
        gate = _dot(hb2, wi_ref[:, j * ch:(j + 1) * ch])
        up = _dot(hb2, wi_ref[:, FFN_HIDDEN + j * ch:FFN_HIDDEN + (j + 1) * ch])
        hj = (gate * _sigmoid(gate) * up).astype(BF16)
        part = _dot(hj, wf_ref[j * ch:(j + 1) * ch, :])
        acc = part if acc is None else acc + part
    x2 = x1 + g2_ref[0] * acc
    if final:
        x2 = _rms(x2) * fg_ref[...]
    o_ref[0] = x2


def _mix_ffn(x, hf, hb, mo, a, c, hg, wo, g1, sh2, sc2, g2, gain2, wi, wf, fg, *, tm):
    b, s, _ = x.shape
    final = fg is not None
    tok = lambda width: pl.BlockSpec((1, tm, width), lambda bi, i: (bi, i, 0))
    vec = pl.BlockSpec((1, 1, D_MODEL), lambda bi, i: (bi, 0, 0))
    in_specs = [tok(D_MODEL), tok(M_WIDTH), tok(M_WIDTH), tok(M_WIDTH), tok(A_WIDTH), tok(C_WIDTH),
                _const_spec((1, M_WIDTH)), _const_spec((D_MODEL, D_MODEL)), vec, vec, vec, vec,
                _const_spec((1, D_MODEL)), _const_spec((D_MODEL, 2 * FFN_HIDDEN)),
                _const_spec((FFN_HIDDEN, D_MODEL))]
    args = [x, hf, hb, mo, a, c, hg, wo, g1, sh2, sc2, g2, gain2, wi, wf]
    if final:
        in_specs.append(_const_spec((1, D_MODEL)))
        args.append(fg)
    return pl.pallas_call(
        functools.partial(_mix_ffn_kernel, final=final, n_chunk=2),
        grid=(b, s // tm),
        in_specs=in_specs,
        out_specs=tok(D_MODEL),
        out_shape=jax.ShapeDtypeStruct((b, s, D_MODEL), F32),
        compiler_params=_cparams(("parallel", "parallel")),
        name="mix_ffn_final" if final else "mix_ffn",
    )(*args)


def _rope_tables(seq):
    half = HEAD_DIM // 4
    freqs = ROPE_THETA ** (-jnp.arange(half, dtype=F32) / half)
    t = jnp.arange(seq)
    sign = jnp.concatenate([-jnp.ones((half,), F32), jnp.ones((half,), F32)])
    cos_parts, sin_parts = [], []
    for pos in (t // GRID_W, t % GRID_W):
        ang = pos.astype(F32)[:, None] * freqs[None, :]
        c, s = jnp.cos(ang), jnp.sin(ang)
        cos_parts.append(jnp.concatenate([c, c], axis=-1))
        sin_parts.append(jnp.concatenate([s, s], axis=-1) * sign)
    cos = jnp.concatenate(cos_parts, axis=-1)
    sin = jnp.concatenate(sin_parts, axis=-1)
    rep = LANES // HEAD_DIM
    return jnp.tile(cos, (1, rep)), jnp.tile(sin, (1, rep))


def _layer_weights(w_in, gate_bias):
    o = np.cumsum((0,) + IN_SPLITS)
    col = lambda i: w_in[:, int(o[i]):int(o[i + 1])]
    scale = HEAD_DIM ** -0.5
    w = jnp.concatenate([col(0), col(1) * scale, col(3), col(5) * scale, col(6), col(7), col(8)],
                        axis=1).astype(BF16)
    gates = col(4).T.reshape(N_DIR, 2, M_HEADS, D_MODEL)
    pad = ((0, 0), (0, SUBLANES - M_HEADS), (0, 0))
    li_rows = jnp.pad(gates[:, 0], pad).reshape(GATE_ROWS, D_MODEL)
    lf_rows = jnp.pad(gates[:, 1], pad).reshape(GATE_ROWS, D_MODEL)
    wt = jnp.concatenate([col(2).T, li_rows, lf_rows], axis=0).astype(BF16)
    bias = jnp.pad(gate_bias.astype(F32), ((0, 0), (0, 0), (0, SUBLANES - M_HEADS)))
    brow = jnp.concatenate([bias[:, 0].reshape(GATE_ROWS), bias[:, 1].reshape(GATE_ROWS)]).reshape(2 * GATE_ROWS, 1)
    return w, wt, brow


def kernel(x, c, ctx, c_ctx, w_mod, b_mod, norm_gain, w_in, mlstm_gate_bias, mlstm_head_gain, attn_sink,
           conv_dw_w, conv_dw_b, conv_ln_g, conv_ln_b, conv_pw_w, w_out, w_ffn_in, w_ffn_out, final_gain):
    b, seq, _ = x.shape
    n_ctx = ctx.shape[1]
    rows = -(-(b + 1) // SUBLANES) * SUBLANES
    cc = jnp.zeros((rows, D_MODEL), F32).at[:b].set(c).at[b].set(c_ctx)
    mod = _modulation(cc, w_mod, b_mod)
    cos_x, sin_x = _rope_tables(seq)
    cos_c, sin_c = cos_x[:n_ctx], sin_x[:n_ctx]
    zero_c = jnp.zeros((b, N_DIR * M_PAIRS, STATE_ROWS, PAIR_W), F32)
    zero_m = jnp.zeros((b, N_DIR, SUBLANES, LANES), F32)
    tm_x = min(512, seq)
    tm_c = min(256, n_ctx)
    tb_x = min(256, seq)
    tb_c = min(256, n_ctx)

    for l in range(DEPTH):
        last = l == DEPTH - 1
        mx_ = mod[l, :b].reshape(b, 1, 6 * D_MODEL)
        mc_ = jnp.broadcast_to(mod[l, b].reshape(1, 1, 6 * D_MODEL), (b, 1, 6 * D_MODEL))
        part = lambda m, i: m[:, :, i * D_MODEL:(i + 1) * D_MODEL]
        w, wt, brow = _layer_weights(w_in[l], mlstm_gate_bias[l])
        gain1 = norm_gain[l, 0].reshape(1, D_MODEL)
        gain2 = norm_gain[l, 1].reshape(1, D_MODEL)
        hg = mlstm_head_gain[l].reshape(1, M_WIDTH)
        sink = attn_sink[l].reshape(1, A_HEADS).astype(F32)
        dw = jnp.pad(conv_dw_w[l], ((0, 1), (0, 0)))
        db = conv_dw_b[l].reshape(1, C_WIDTH)
        lg = conv_ln_g[l].reshape(1, C_WIDTH)
        lb = conv_ln_b[l].reshape(1, C_WIDTH)
        pw = conv_pw_w[l].astype(BF16)
        wo = w_out[l].astype(BF16)
        wi = w_ffn_in[l].astype(BF16)
        wf = w_ffn_out[l].astype(BF16)

        px = _inproj(x, gain1, part(mx_, 0), part(mx_, 1), w, wt, brow, cos_x, sin_x, rope=True, tm=tm_x)
        pc = _inproj(ctx, gain1, part(mc_, 0), part(mc_, 1), w, wt, brow, cos_c, sin_c, rope=False, tm=tm_c)
        mq, mk, mo, vt, li, lf, aq, ak, av, cu = px
        cmq, cmk, cmo, cvt, cli, clf, caq, cak, cav, ccu = pc

        chf, chb, cst, mst = _mlstm(cmq, cmk, cvt, cli, clf, zero_c, zero_m, tb=tb_c)
        hf, hb, _, _ = _mlstm(mq, mk, vt, li, lf, cst, mst, tb=tb_x)
        a_x = _attention(aq, ak, av, cak, cav, sink, local=True)
        c_x = _conv(cu, dw, db, lg, lb, pw, tc=tm_x)
        x = _mix_ffn(x, hf, hb, mo, a_x, c_x, hg, wo, part(mx_, 2), part(mx_, 3), part(mx_, 4), part(mx_, 5),
                     gain2, wi, wf, final_gain.reshape(1, D_MODEL) if last else None, tm=tm_x)
        if not last:
            a_c = _attention(caq, cak, cav, cak, cav, sink, local=False)
            c_c = _conv(ccu, dw, db, lg, lb, pw, tc=tm_c)
            ctx = _mix_ffn(ctx, chf, chb, cmo, a_c, c_c, hg, wo, part(mc_, 2), part(mc_, 3), part(mc_, 4),
                           part(mc_, 5), gain2, wi, wf, None, tm=tm_c)
    return x
```

```python
import functools

import numpy as np
import jax
import jax.numpy as jnp
from jax import lax
from jax.experimental import pallas as pl
from jax.experimental.pallas import tpu as pltpu

F32 = jnp.float32
BF16 = jnp.bfloat16

D_MODEL = 1024
DEPTH = 2
GRID_W = 64
HEAD_DIM = 64
M_WIDTH = D_MODEL // 4
M_HEADS = M_WIDTH // HEAD_DIM
N_DIR = 2
A_WIDTH = D_MODEL // 2
A_HEADS = A_WIDTH // HEAD_DIM
A_KV_HEADS = A_HEADS // 4
A_GROUP = A_HEADS // A_KV_HEADS
KV_WIDTH = A_KV_HEADS * HEAD_DIM
WINDOW = 128
ROPE_THETA = 10000.0
C_WIDTH = D_MODEL // 4
CONV_K = 31
CONV_HALO = 16
FFN_HIDDEN = ((8 * D_MODEL + 3 * 256 - 1) // (3 * 256)) * 256
IN_SPLITS = (M_WIDTH, M_WIDTH, M_WIDTH, M_WIDTH, N_DIR * 2 * M_HEADS,
             A_WIDTH, KV_WIDTH, KV_WIDTH, 2 * C_WIDTH)
EPS = 1e-6
NEG_INF = -1e30

LANES = 128
SUBLANES = 8
VMEM_LIMIT = 52 * 1024 * 1024

C_MQ = 0
C_MK = C_MQ + M_WIDTH
C_MO = C_MK + M_WIDTH
C_AQ = C_MO + M_WIDTH
C_AK = C_AQ + A_WIDTH
C_AV = C_AK + KV_WIDTH
C_CU = C_AV + KV_WIDTH
N_COLS = C_CU + 2 * C_WIDTH
GATE_ROWS = N_DIR * SUBLANES
R_LI = M_WIDTH
R_LF = R_LI + GATE_ROWS
NT_ROWS = R_LF + GATE_ROWS
M_PAIRS = M_HEADS // 2
PAIR_W = 2 * HEAD_DIM
STATE_ROWS = PAIR_W + SUBLANES
M_CHUNK = LANES


def _cparams(sem):
    return pltpu.CompilerParams(dimension_semantics=sem, vmem_limit_bytes=VMEM_LIMIT)


def _const_spec(shape):
    nd = len(shape)
    return pl.BlockSpec(shape, lambda *_: (0,) * nd, pipeline_mode=pl.Buffered(1))


def _log_sigmoid(x):
    return jnp.minimum(x, 0.0) - jnp.log1p(jnp.exp(-jnp.abs(x)))


def _sigmoid(x):
    return jax.nn.sigmoid(x)


def _rms(x):
    return x * lax.rsqrt(jnp.mean(x * x, axis=-1, keepdims=True) + EPS)


def _dot(a, b):
    return jnp.dot(a, b, preferred_element_type=F32)


def _dot_nt(a, b):
    return lax.dot_general(a, b, (((1,), (1,)), ((), ())), preferred_element_type=F32)


def _split_bf16(x, parts):
    out = []
    r = x
    for _ in range(parts):
        p = r.astype(BF16)
        out.append(p)
        r = r - p.astype(F32)
    return out


def _rows(x, i, n):
    return jnp.broadcast_to(x[i:i + 1], (n, x.shape[1]))


def _mod_kernel(c_ref, w_ref, b_ref, o_ref):
    cc = c_ref[...]
    s = (cc * _sigmoid(cc)).astype(BF16)
    o_ref[0] = _dot(s, w_ref[0].astype(BF16)) + b_ref[0]


def _modulation(cc, w_mod, b_mod):
    rows = cc.shape[0]
    n = w_mod.shape[-1]
    tn = 1536
    return pl.pallas_call(
        _mod_kernel,
        grid=(DEPTH, n // tn),
        in_specs=[pl.BlockSpec((rows, D_MODEL), lambda l, j: (0, 0)),
                  pl.BlockSpec((1, D_MODEL, tn), lambda l, j: (l, 0, j)),
                  pl.BlockSpec((1, 1, tn), lambda l, j: (l, 0, j))],
        out_specs=pl.BlockSpec((1, rows, tn), lambda l, j: (l, 0, j)),
        out_shape=jax.ShapeDtypeStruct((DEPTH, rows, n), F32),
        compiler_params=_cparams(("arbitrary", "arbitrary")),
        name="modulation",
    )(cc, w_mod, b_mod.reshape(DEPTH, 1, n))


def _inproj_kernel(x_ref, gain_ref, shift_ref, scale_ref, w_ref, wt_ref, brow_ref, cos_ref, sin_ref,
                   mq_ref, mk_ref, mo_ref, vt_ref, li_ref, lf_ref, aq_ref, ak_ref, av_ref, cu_ref, *, rope):
    xf = x_ref[0]
    h = _rms(xf) * gain_ref[...]
    h = h * (1.0 + scale_ref[0]) + shift_ref[0]
    hb = h.astype(BF16)
    p = _dot(hb, w_ref[...])
    pt = _dot_nt(wt_ref[...], hb)
    mq_ref[0] = p[:, C_MQ:C_MK].astype(BF16)
    mk_ref[0] = p[:, C_MK:C_MO].astype(BF16)
    mo_ref[0] = p[:, C_MO:C_AQ]
    av_ref[0] = p[:, C_AV:C_CU].astype(BF16)
    cu_ref[0] = p[:, C_CU:N_COLS]
    vt_ref[0] = pt[0:M_WIDTH].astype(BF16)
    g = pt[R_LI:NT_ROWS] + brow_ref[...]
    li_ref[0] = g[0:GATE_ROWS]
    lf = g[GATE_ROWS:2 * GATE_ROWS]
    r = lax.broadcasted_iota(jnp.int32, lf.shape, 0)
    lf_ref[0] = jnp.where(r % SUBLANES < M_HEADS, _log_sigmoid(lf), 0.0)
    if rope:
        cos = cos_ref[...]
        sin = sin_ref[...]
        lane = lax.broadcasted_iota(jnp.int32, cos.shape, 1)
        first = (lane % 32) < 16

        def rot(z):
            sw = jnp.where(first, pltpu.roll(z, LANES - 16, 1), pltpu.roll(z, 16, 1))
            return z * cos + sw * sin

        for j in range(A_WIDTH // LANES):
            aq_ref[0, :, j * LANES:(j + 1) * LANES] = rot(
                p[:, C_AQ + j * LANES:C_AQ + (j + 1) * LANES]).astype(BF16)
        ak_ref[0] = rot(p[:, C_AK:C_AV]).astype(BF16)
    else:
        aq_ref[0] = p[:, C_AQ:C_AK].astype(BF16)
        ak_ref[0] = p[:, C_AK:C_AV].astype(BF16)


def _inproj(x, gain, shift, scale, w, wt, brow, cos, sin, *, rope, tm):
    b, s, _ = x.shape
    tok = lambda width: pl.BlockSpec((1, tm, width), lambda bi, i: (bi, i, 0))
    vec = pl.BlockSpec((1, 1, D_MODEL), lambda bi, i: (bi, 0, 0))
    tab = pl.BlockSpec((tm, LANES), lambda bi, i: (i, 0))
    tr = lambda rows: pl.BlockSpec((1, rows, tm), lambda bi, i: (bi, 0, i))
    tok_shape = lambda width, dt: jax.ShapeDtypeStruct((b, s, width), dt)
    tr_shape = lambda rows, dt: jax.ShapeDtypeStruct((b, rows, s), dt)
    out_specs = [tok(M_WIDTH), tok(M_WIDTH), tok(M_WIDTH), tr(M_WIDTH), tr(GATE_ROWS), tr(GATE_ROWS),
                 tok(A_WIDTH), tok(KV_WIDTH), tok(KV_WIDTH), tok(2 * C_WIDTH)]
    out_shape = [tok_shape(M_WIDTH, BF16), tok_shape(M_WIDTH, BF16), tok_shape(M_WIDTH, F32), tr_shape(M_WIDTH, BF16),
                 tr_shape(GATE_ROWS, F32), tr_shape(GATE_ROWS, F32),
                 tok_shape(A_WIDTH, BF16), tok_shape(KV_WIDTH, BF16), tok_shape(KV_WIDTH, BF16),
                 tok_shape(2 * C_WIDTH, F32)]
    return pl.pallas_call(
        functools.partial(_inproj_kernel, rope=rope),
        grid=(b, s // tm),
        in_specs=[tok(D_MODEL), _const_spec((1, D_MODEL)), vec, vec,
                  _const_spec((D_MODEL, N_COLS)), _const_spec((NT_ROWS, D_MODEL)),
                  _const_spec((2 * GATE_ROWS, 1)), tab, tab],
        out_specs=out_specs,
        out_shape=out_shape,
        compiler_params=_cparams(("parallel", "parallel")),
        name="inproj_rope" if rope else "inproj_ctx",
    )(x, gain, shift, scale, w, wt, brow, cos, sin)


def _mlstm_kernel(qf_ref, kf_ref, vtf_ref, lif_ref, lff_ref, qb_ref, kb_ref, vtb_ref, lib_ref, lfb_ref,
                  c0_ref, m0_ref, hf_ref, hb_ref, cst_ref, mst_ref, *, nsub):
    t = M_CHUNK

    @pl.when(pl.program_id(1) == 0)
    def _():
        cst_ref[...] = c0_ref[...]
        mst_ref[...] = m0_ref[...]

    key = lax.broadcasted_iota(jnp.int32, (t, t), 0)
    qry = lax.broadcasted_iota(jnp.int32, (t, t), 1)
    key2 = lax.broadcasted_iota(jnp.int32, (2 * t, t), 0) % t
    qry2 = lax.broadcasted_iota(jnp.int32, (2 * t, t), 1)
    first_head_lanes = lax.broadcasted_iota(jnp.int32, (t, PAIR_W), 1) < HEAD_DIM
    first_head_rows = lax.broadcasted_iota(jnp.int32, (PAIR_W, t), 0) < HEAD_DIM
    den_r = lax.broadcasted_iota(jnp.int32, (SUBLANES, 2 * t), 0)
    den_c = lax.broadcasted_iota(jnp.int32, (SUBLANES, 2 * t), 1) // t
    st_r = lax.broadcasted_iota(jnp.int32, (STATE_ROWS, PAIR_W), 0)
    st_c = lax.broadcasted_iota(jnp.int32, (STATE_ROWS, PAIR_W), 1) // HEAD_DIM
    dirs = ((qf_ref, kf_ref, vtf_ref, lif_ref, lff_ref, hf_ref),
            (qb_ref, kb_ref, vtb_ref, lib_ref, lfb_ref, hb_ref))
    chains = [(d, p) for d in range(N_DIR) for p in range(M_PAIRS)]
    visible = [key <= qry, key >= qry]
    visible2 = [key2 <= qry2, key2 >= qry2]
    ones_st = [jnp.where(v, 1.0, 0.0).astype(BF16) for v in visible]
    state = {dp: cst_ref[0, dp[0] * M_PAIRS + dp[1]] for dp in chains}
    m_state = [mst_ref[0, d] for d in range(N_DIR)]

    gates, pairs = {}, {}
    for step in range(nsub):
        for d in range(N_DIR):
            rev = d == 1
            j = nsub - 1 - step if rev else step
            tok = slice(j * t, (j + 1) * t)
            gate_rows = slice(d * SUBLANES, (d + 1) * SUBLANES)
            li8 = dirs[d][3][0, gate_rows, tok]
            lf8 = dirs[d][4][0, gate_rows, tok]
            a8 = sum(_dot(part, ones_st[d]) for part in _split_bf16(lf8, 3))
            b8 = li8 - a8
            a_end = jnp.broadcast_to(a8[:, 0:1] if rev else a8[:, t - 1:t], (SUBLANES, t))
            g8 = a_end + b8
            g_max = jnp.broadcast_to(jnp.max(g8, axis=1, keepdims=True), (SUBLANES, t))
            b_col = jnp.transpose(jnp.concatenate([b8, jnp.zeros((t - SUBLANES, t), F32)], axis=0))
            gates[step, d] = (a8, a_end, g8, g_max)
            for p in range(M_PAIRS):
                h0, h1 = 2 * p, 2 * p + 1
                lanes = slice(p * PAIR_W, (p + 1) * PAIR_W)
                q01 = dirs[d][0][0, tok, lanes].astype(BF16)
                k01 = dirs[d][1][0, tok, lanes]
                vt01 = dirs[d][2][0, lanes, tok]
                kbd = jnp.concatenate([jnp.where(first_head_lanes, k01, 0.0),
                                       jnp.where(first_head_lanes, 0.0, k01)], axis=0).astype(BF16)
                s2 = _dot_nt(kbd, q01)
                bc = jnp.concatenate([jnp.broadcast_to(b_col[:, h0:h0 + 1], (t, t)),
                                      jnp.broadcast_to(b_col[:, h1:h1 + 1], (t, t))], axis=0)
                bc = jnp.where(visible2[d], bc, -jnp.inf)
                cm = [jnp.max(bc[i * t:(i + 1) * t], axis=0, keepdims=True) for i in range(2)]
                vbd =jnp.concatenate([jnp.where(first_head_rows, vt01, 0.0),
                                       jnp.where(first_head_rows, 0.0, vt01)], axis=1)
                den_rows = jnp.where(((den_r == h0) & (den_c == 0)) | ((den_r == h1) & (den_c == 1)), 1.0, 0.0)
                lhs = jnp.concatenate([vbd, den_rows], axis=0).astype(BF16)
                pairs[step, d, p] = (tok, lanes, q01, k01.astype(BF16), vt01, s2, bc, cm, lhs)

    row8 = lax.broadcasted_iota(jnp.int32, (SUBLANES, t), 0)
    for step in range(nsub):
        small = {}
        for d in range(N_DIR):
            a8, a_end, g8, g_max = gates[step, d]
            m_prev = m_state[d]
            m_new = jnp.maximum(a_end + m_prev, g_max)
            decay8 = jnp.exp(a_end + m_prev - m_new)
            wg8 = jnp.exp(g8 - m_new)
            m_state[d] = m_new
            small[d] = (a8, m_prev, decay8, wg8)
        weighted = {}
        for d, p in chains:
            tok, lanes, q01, k01b, vt01, s2, bc, cm, lhs = pairs[step, d, p]
            a8, m_prev = small[d][0:2]
            mx = [jnp.maximum(cm[i], m_prev[2 * p + i:2 * p + i + 1]) for i in range(2)]
            w = [jnp.exp(m_prev[2 * p + i:2 * p + i + 1] - mx[i]) for i in range(2)]
            e = [jnp.exp(-(a8[2 * p + i:2 * p + i + 1] + mx[i])) for i in range(2)]
            mr = jnp.concatenate([jnp.broadcast_to(mx[0], (t, t)), jnp.broadcast_to(mx[1], (t, t))], axis=0)
            weighted[d, p] = ((s2 * jnp.exp(bc - mr)).astype(BF16), w, e)
        for d, p in chains:
            tok, lanes, q01, k01b, vt01, s2, bc, cm, lhs = pairs[step, d, p]
            a8, m_prev, decay8, wg8 = small[d]
            sp, w, e = weighted[d, p]
            h0, h1 = 2 * p, 2 * p + 1
            st = state[d, p]
            intra = _dot(lhs, sp)
            inter = _dot_nt(st.astype(BF16), q01)
            w8 = jnp.where(row8 == h0, w[0], jnp.where(row8 == h1, w[1], 0.0))
            w_rows = jnp.concatenate([jnp.broadcast_to(w[0], (HEAD_DIM, t)), jnp.broadcast_to(w[1], (HEAD_DIM, t)),
                                      w8], axis=0)
            tot = w_rows * inter + intra
            den = jnp.concatenate([_rows(tot, PAIR_W + h0, HEAD_DIM), _rows(tot, PAIR_W + h1, HEAD_DIM)], axis=0)
            e_rows = jnp.concatenate([jnp.broadcast_to(e[0], (HEAD_DIM, t)), jnp.broadcast_to(e[1], (HEAD_DIM, t))],
                                     axis=0)
            h_t = tot[0:PAIR_W] / jnp.maximum(jnp.abs(den), e_rows)
            dirs[d][5][0, tok, lanes] = jnp.transpose(h_t)
            wg_rows = jnp.concatenate([_rows(wg8, h0, HEAD_DIM), _rows(wg8, h1, HEAD_DIM)], axis=0)
            upd = _dot(jnp.concatenate([vt01 * wg_rows, wg8], axis=0).astype(BF16), k01b)
            keep = (st_r // HEAD_DIM == st_c) | (st_r - PAIR_W == st_c + h0)
            dec = jnp.concatenate([_rows(decay8, h0, HEAD_DIM), _rows(decay8, h1, HEAD_DIM), decay8], axis=0)
            state[d, p] = dec * st + jnp.where(keep, upd, 0.0)

    for d, p in chains:
        cst_ref[0, d * M_PAIRS + p] = state[d, p]
    for d in range(N_DIR):
        mst_ref[0, d] = m_state[d]


def _mlstm(mq, mk, vt, li, lf, c0, m0, *, tb):
    b, s, _ = mq.shape
    nc = s // tb
    fwd = lambda width: pl.BlockSpec((1, tb, width), lambda bi, c: (bi, c, 0))
    bwd = lambda width: pl.BlockSpec((1, tb, width), lambda bi, c: (bi, nc - 1 - c, 0))
    fwd_t = lambda rows: pl.BlockSpec((1, rows, tb), lambda bi, c: (bi, 0, c))
    bwd_t = lambda rows: pl.BlockSpec((1, rows, tb), lambda bi, c: (bi, 0, nc - 1 - c))
    st_c = pl.BlockSpec((1, N_DIR * M_PAIRS, STATE_ROWS, PAIR_W), lambda bi, c: (bi, 0, 0, 0))
    st_m = pl.BlockSpec((1, N_DIR, SUBLANES, LANES), lambda bi, c: (bi, 0, 0, 0))
    return pl.pallas_call(
        functools.partial(_mlstm_kernel, nsub=tb // M_CHUNK),
        grid=(b, nc),
        in_specs=[fwd(M_WIDTH), fwd(M_WIDTH), fwd_t(M_WIDTH), fwd_t(GATE_ROWS), fwd_t(GATE_ROWS),
                  bwd(M_WIDTH), bwd(M_WIDTH), bwd_t(M_WIDTH), bwd_t(GATE_ROWS), bwd_t(GATE_ROWS),
                  st_c, st_m],
        out_specs=[fwd(M_WIDTH), bwd(M_WIDTH), st_c, st_m],
        out_shape=[jax.ShapeDtypeStruct((b, s, M_WIDTH), F32), jax.ShapeDtypeStruct((b, s, M_WIDTH), F32),
                   jax.ShapeDtypeStruct(c0.shape, F32), jax.ShapeDtypeStruct(m0.shape, F32)],
        compiler_params=_cparams(("parallel", "arbitrary")),
        name="mlstm_scan",
    )(mq, mk, vt, li, lf, mq, mk, vt, li, lf, c0, m0)


def _attn_kernel(*refs, seq, nq, local):
    tq = WINDOW
    if local:
        q_ref, kp_ref, kc_ref, kn_ref, vp_ref, vc_ref, vn_ref, kx_ref, vx_ref, sink_ref, o_ref = refs
        k_blocks = [kp_ref[0]] + [kc_ref[0, i * tq:(i + 1) * tq] for i in range(nq)] + [kn_ref[0]]
        v_blocks = [vp_ref[0]] + [vc_ref[0, i * tq:(i + 1) * tq] for i in range(nq)] + [vn_ref[0]]
        start = pl.program_id(1) * (nq * tq)
        row = lax.broadcasted_iota(jnp.int32, (tq, 2 * tq), 0)
        side = lax.broadcasted_iota(jnp.int32, (tq, 2 * tq), 1)
    else:
        q_ref, kx_ref, vx_ref, sink_ref, o_ref = refs
    ones = jnp.ones((kx_ref.shape[1] + (3 * tq if local else 0), HEAD_DIM), BF16)
    units = []
    for blk in range(nq):
        if local:
            k_cat = jnp.concatenate([k_blocks[blk], k_blocks[blk + 2], k_blocks[blk + 1], kx_ref[0]], axis=0)
            v_cat = jnp.concatenate([v_blocks[blk], v_blocks[blk + 2], v_blocks[blk + 1], vx_ref[0]], axis=0)
            qpos = start + blk * tq + row
            kpos = start + blk * tq + jnp.where(side < tq, side - tq, side)
            ok = (jnp.abs(qpos - kpos) <= WINDOW) & (kpos >= 0) & (kpos < seq)
            bias = jnp.concatenate([jnp.where(ok, 0.0, NEG_INF)] * A_GROUP, axis=0)
        else:
            k_cat, v_cat, bias = kx_ref[0], vx_ref[0], None
        for g in range(A_KV_HEADS):
            kg = k_cat[:, g * HEAD_DIM:(g + 1) * HEAD_DIM]
            vg = jnp.concatenate([v_cat[:, g * HEAD_DIM:(g + 1) * HEAD_DIM], ones], axis=1)
            units.append((blk, g * A_GROUP, kg, vg, bias))
    scores, sinks = [], []
    for blk, hq0, kg, vg, bias in units:
        qg = jnp.concatenate([q_ref[0, blk * tq:(blk + 1) * tq, (hq0 + j) * HEAD_DIM:(hq0 + j + 1) * HEAD_DIM]
                              for j in range(A_GROUP)], axis=0)
        s = _dot_nt(qg, kg)
        if local:
            s = jnp.concatenate([s[:, :2 * tq] + bias, s[:, 2 * tq:]], axis=1)
        scores.append(s)
        sinks.append(jnp.concatenate(
            [jnp.broadcast_to(sink_ref[0:1, hq0 + j:hq0 + j + 1], (tq, LANES)) for j in range(A_GROUP)], axis=0))
    maxes = [jnp.maximum(jnp.broadcast_to(jnp.max(s, axis=1, keepdims=True), sk.shape), sk)
             for s, sk in zip(scores, sinks)]
    probs = [jnp.concatenate([jnp.exp(s[:, i:i + LANES] - m).astype(BF16) for i in range(0, s.shape[1], LANES)],
                             axis=1) for s, m in zip(scores, maxes)]
    outs = [_dot(p, u[3]) for p, u in zip(probs, units)]
    for ol, sk, m, (blk, hq0, _, _, _) in zip(outs, sinks, maxes, units):
        o = (ol[:, 0:HEAD_DIM] / (ol[:, HEAD_DIM:] + jnp.exp(sk - m)[:, HEAD_DIM:])).astype(o_ref.dtype)
        for j in range(A_GROUP):
            o_ref[0, blk * tq:(blk + 1) * tq, (hq0 + j) * HEAD_DIM:(hq0 + j + 1) * HEAD_DIM] = o[j * tq:(j + 1) * tq]


def _attention(aq, ak, av, kx, vx, sink, *, local):
    b, s, _ = aq.shape
    n_ctx = kx.shape[1]
    tq = WINDOW
    nq = 2
    nb = s // tq
    qs = pl.BlockSpec((1, nq * tq, A_WIDTH), lambda bi, i: (bi, i, 0))
    cx = pl.BlockSpec((1, n_ctx, KV_WIDTH), lambda bi, i: (bi, 0, 0))
    sk = pl.BlockSpec((1, A_HEADS), lambda bi, i: (0, 0))
    if local:
        prv = pl.BlockSpec((1, tq, KV_WIDTH), lambda bi, i: (bi, jnp.maximum(nq * i - 1, 0), 0))
        cur = pl.BlockSpec((1, nq * tq, KV_WIDTH), lambda bi, i: (bi, i, 0))
        nxt = pl.BlockSpec((1, tq, KV_WIDTH), lambda bi, i: (bi, jnp.minimum(nq * (i + 1), nb - 1), 0))
        in_specs = [qs, prv, cur, nxt, prv, cur, nxt, cx, cx, sk]
        args = (aq, ak, ak, ak, av, av, av, kx, vx, sink)
    else:
        in_specs = [qs, cx, cx, sk]
        args = (aq, kx, vx, sink)
    return pl.pallas_call(
        functools.partial(_attn_kernel, seq=s, nq=nq, local=local),
        grid=(b, nb // nq),
        in_specs=in_specs,
        out_specs=qs,
        out_shape=jax.ShapeDtypeStruct((b, s, A_WIDTH), BF16),
        compiler_params=_cparams(("parallel", "parallel")),
        name="window_attn" if local else "ctx_attn",
    )(*args)


def _conv_kernel(up_ref, uc_ref, un_ref, dw_ref, db_ref, lg_ref, lb_ref, pw_ref, o_ref, ybuf, *, tc, sub):
    i = pl.program_id(1)
    nblk = pl.num_programs(1)
    span = tc + 2 * CONV_HALO - SUBLANES

    def glu(u):
        return u[:, :C_WIDTH] * _sigmoid(u[:, C_WIDTH:])

    ybuf[0, 0:CONV_HALO] = jnp.where(i > 0, glu(up_ref[0]), 0.0)
    ybuf[0, CONV_HALO:CONV_HALO + tc] = glu(uc_ref[0])
    ybuf[0, CONV_HALO + tc:2 * CONV_HALO + tc] = jnp.where(i < nblk - 1, glu(un_ref[0]), 0.0)
    for r in range(1, SUBLANES):
        ybuf[r, 0:span] = ybuf[0, r:r + span]
    off = CONV_HALO - CONV_K // 2
    for r0 in range(0, tc, sub):
        acc = jnp.zeros((sub, C_WIDTH), F32)
        for k in range(CONV_K):
            r = (off + k) % SUBLANES
            base = r0 + off + k - r
            acc = acc + ybuf[r, base:base + sub, :] * dw_ref[k:k + 1, :]
        y = acc + db_ref[...]
        mu = jnp.mean(y, axis=-1, keepdims=True)
        yc = y - mu
        var = jnp.mean(yc * yc, axis=-1, keepdims=True)
        z = yc * lax.rsqrt(var + EPS) * lg_ref[...] + lb_ref[...]
        z = z * _sigmoid(z)
        o_ref[0, r0:r0 + sub, :] = _dot(z.astype(BF16), pw_ref[...]).astype(o_ref.dtype)


def _conv(cu, dw, db, lg, lb, pw, *, tc):
    b, s, _ = cu.shape
    nh = s // CONV_HALO
    r = tc // CONV_HALO
    cur = pl.BlockSpec((1, tc, 2 * C_WIDTH), lambda bi, i: (bi, i, 0))
    prv = pl.BlockSpec((1, CONV_HALO, 2 * C_WIDTH), lambda bi, i: (bi, jnp.maximum(i * r - 1, 0), 0))
    nxt = pl.BlockSpec((1, CONV_HALO, 2 * C_WIDTH), lambda bi, i: (bi, jnp.minimum((i + 1) * r, nh - 1), 0))
    return pl.pallas_call(
        functools.partial(_conv_kernel, tc=tc, sub=64),
        grid=(b, s // tc),
        in_specs=[prv, cur, nxt, _const_spec((CONV_K + 1, C_WIDTH)), _const_spec((1, C_WIDTH)),
                  _const_spec((1, C_WIDTH)), _const_spec((1, C_WIDTH)), _const_spec((C_WIDTH, C_WIDTH))],
        out_specs=pl.BlockSpec((1, tc, C_WIDTH), lambda bi, i: (bi, i, 0)),
        out_shape=jax.ShapeDtypeStruct((b, s, C_WIDTH), BF16),
        scratch_shapes=[pltpu.VMEM((SUBLANES, tc + 2 * CONV_HALO, C_WIDTH), F32)],
        compiler_params=_cparams(("parallel", "parallel")),
        name="conformer_conv",
    )(cu, cu, cu, dw, db, lg, lb, pw)


def _mix_ffn_kernel(*refs, final, n_chunk):
    (x_ref, hf_ref, hb_ref, mo_ref, a_ref, c_ref, hg_ref, wo_ref, g1_ref, sh2_ref, sc2_ref, g2_ref,
     gain2_ref, wi_ref, wf_ref) = refs[:15]
    fg_ref = refs[15] if final else None
    o_ref = refs[-1]
    hs = hf_ref[0] + hb_ref[0]
    ri = lax.broadcasted_iota(jnp.int32, (M_WIDTH, M_WIDTH), 0) // HEAD_DIM
    ci = lax.broadcasted_iota(jnp.int32, (M_WIDTH, M_WIDTH), 1) // HEAD_DIM
    same_head = jnp.where(ri == ci, 1.0, 0.0).astype(BF16)
    ms = sum(_dot(part, same_head) for part in _split_bf16(hs * hs, 2)) * (1.0 / HEAD_DIM)
    mx = hs * lax.rsqrt(ms + EPS) * hg_ref[...] * _sigmoid(mo_ref[0])
    y = (_dot(mx.astype(BF16), wo_ref[0:M_WIDTH, :])
         + _dot(a_ref[0].astype(BF16), wo_ref[M_WIDTH:M_WIDTH + A_WIDTH, :])
         + _dot(c_ref[0].astype(BF16), wo_ref[M_WIDTH + A_WIDTH:, :]))
    x1 = x_ref[0] + g1_ref[0] * y
    h2 = (_rms(x1) * gain2_ref[...]) * (1.0 + sc2_ref[0]) + sh2_ref[0]
    hb2 = h2.astype(BF16)
    ch = FFN_HIDDEN // n_chunk
    acc = None
    for j in range(n_chunk):
        gate = _dot(hb2, wi_ref[:, j * ch:(j + 1) * ch])
        up = _dot(hb2, wi_ref[:, FFN_HIDDEN + j * ch:FFN_HIDDEN + (j + 1) * ch])
        hj = (gate * _sigmoid(gate) * up).astype(BF16)
        part = _dot(hj, wf_ref[j * ch:(j + 1) * ch, :])
        acc = part if acc is None else acc + part
    x2 = x1 + g2_ref[0] * acc
    if final:
        x2 = _rms(x2) * fg_ref[...]
    o_ref[0] = x2


def _mix_ffn(x, hf, hb, mo, a, c, hg, wo, g1, sh2, sc2, g2, gain2, wi, wf, fg, *, tm):
    b, s, _ = x.shape
    final = fg is not None
    tok = lambda width: pl.BlockSpec((1, tm, width), lambda bi, i: (bi, i, 0))
    vec = pl.BlockSpec((1, 1, D_MODEL), lambda bi, i: (bi, 0, 0))
    in_specs = [tok(D_MODEL), tok(M_WIDTH), tok(M_WIDTH), tok(M_WIDTH), tok(A_WIDTH), tok(C_WIDTH),
                _const_spec((1, M_WIDTH)), _const_spec((D_MODEL, D_MODEL)), vec, vec, vec, vec,
                _const_spec((1, D_MODEL)), _const_spec((D_MODEL, 2 * FFN_HIDDEN)),
                _const_spec((FFN_HIDDEN, D_MODEL))]
    args = [x, hf, hb, mo, a, c, hg, wo, g1, sh2, sc2, g2, gain2, wi, wf]
    if final:
        in_specs.append(_const_spec((1, D_MODEL)))
        args.append(fg)
    return pl.pallas_call(
        functools.partial(_mix_ffn_kernel, final=final, n_chunk=2),
        grid=(b, s // tm),
        in_specs=in_specs,
        out_specs=tok(D_MODEL),
        out_shape=jax.ShapeDtypeStruct((b, s, D_MODEL), F32),
        compiler_params=_cparams(("parallel", "parallel")),
        name="mix_ffn_final" if final else "mix_ffn",
    )(*args)


def _rope_tables(seq):
    half = HEAD_DIM // 4
    freqs = ROPE_THETA ** (-jnp.arange(half, dtype=F32) / half)
    t = jnp.arange(seq)
    sign = jnp.concatenate([-jnp.ones((half,), F32), jnp.ones((half,), F32)])
    cos_parts, sin_parts = [], []
    for pos in (t // GRID_W, t % GRID_W):
        ang = pos.astype(F32)[:, None] * freqs[None, :]
        c, s = jnp.cos(ang), jnp.sin(ang)
        cos_parts.append(jnp.concatenate([c, c], axis=-1))
        sin_parts.append(jnp.concatenate([s, s], axis=-1) * sign)
    cos = jnp.concatenate(cos_parts, axis=-1)
    sin = jnp.concatenate(sin_parts, axis=-1)
    rep = LANES // HEAD_DIM
    return jnp.tile(cos, (1, rep)), jnp.tile(sin, (1, rep))


def _layer_weights(w_in, gate_bias):
    o = np.cumsum((0,) + IN_SPLITS)
    col = lambda i: w_in[:, int(o[i]):int(o[i + 1])]
    scale = HEAD_DIM ** -0.5
    w = jnp.concatenate([col(0), col(1) * scale, col(3), col(5) * scale, col(6), col(7), col(8)],
                        axis=1).astype(BF16)
    gates = col(4).T.reshape(N_DIR, 2, M_HEADS, D_MODEL)
    pad = ((0, 0), (0, SUBLANES - M_HEADS), (0, 0))
    li_rows = jnp.pad(gates[:, 0], pad).reshape(GATE_ROWS, D_MODEL)
    lf_rows = jnp.pad(gates[:, 1], pad).reshape(GATE_ROWS, D_MODEL)
    wt = jnp.concatenate([col(2).T, li_rows, lf_rows], axis=0).astype(BF16)
    bias = jnp.pad(gate_bias.astype(F32), ((0, 0), (0, 0), (0, SUBLANES - M_HEADS)))
    brow = jnp.concatenate([bias[:, 0].reshape(GATE_ROWS), bias[:, 1].reshape(GATE_ROWS)]).reshape(2 * GATE_ROWS, 1)
    return w, wt, brow


def kernel(x, c, ctx, c_ctx, w_mod, b_mod, norm_gain, w_in, mlstm_gate_bias, mlstm_head_gain, attn_sink,
           conv_dw_w, conv_dw_b, conv_ln_g, conv_ln_b, conv_pw_w, w_out, w_ffn_in, w_ffn_out, final_gain):
    b, seq, _ = x.shape
    n_ctx = ctx.shape[1]
    rows = -(-(b + 1) // SUBLANES) * SUBLANES
    cc = jnp.zeros((rows, D_MODEL), F32).at[:b].set(c).at[b].set(c_ctx)
    mod = _modulation(cc, w_mod, b_mod)
    cos_x, sin_x = _rope_tables(seq)
    cos_c, sin_c = cos_x[:n_ctx], sin_x[:n_ctx]
    zero_c = jnp.zeros((b, N_DIR * M_PAIRS, STATE_ROWS, PAIR_W), F32)
    zero_m = jnp.zeros((b, N_DIR, SUBLANES, LANES), F32)
    tm_x = min(512, seq)
    tm_c = min(256, n_ctx)
    tb_x = min(1024, seq)
    tb_c = min(256, n_ctx)

    for l in range(DEPTH):
        last = l == DEPTH - 1
        mx_ = mod[l, :b].reshape(b, 1, 6 * D_MODEL)
        mc_ = jnp.broadcast_to(mod[l, b].reshape(1, 1, 6 * D_MODEL), (b, 1, 6 * D_MODEL))
        part = lambda m, i: m[:, :, i * D_MODEL:(i + 1) * D_MODEL]
        w, wt, brow = _layer_weights(w_in[l], mlstm_gate_bias[l])
        gain1 = norm_gain[l, 0].reshape(1, D_MODEL)
        gain2 = norm_gain[l, 1].reshape(1, D_MODEL)
        hg = mlstm_head_gain[l].reshape(1, M_WIDTH)
        sink = attn_sink[l].reshape(1, A_HEADS).astype(F32)
        dw = jnp.pad(conv_dw_w[l], ((0, 1), (0, 0)))
        db = conv_dw_b[l].reshape(1, C_WIDTH)
        lg = conv_ln_g[l].reshape(1, C_WIDTH)
        lb = conv_ln_b[l].reshape(1, C_WIDTH)
        pw = conv_pw_w[l].astype(BF16)
        wo = w_out[l].astype(BF16)
        wi = w_ffn_in[l].astype(BF16)
        wf = w_ffn_out[l].astype(BF16)

        px = _inproj(x, gain1, part(mx_, 0), part(mx_, 1), w, wt, brow, cos_x, sin_x, rope=True, tm=tm_x)
        pc = _inproj(ctx, gain1, part(mc_, 0), part(mc_, 1), w, wt, brow, cos_c, sin_c, rope=False, tm=tm_c)
        mq, mk, mo, vt, li, lf, aq, ak, av, cu = px
        cmq, cmk, cmo, cvt, cli, clf, caq, cak, cav, ccu = pc

        chf, chb, cst, mst = _mlstm(cmq, cmk, cvt, cli, clf, zero_c, zero_m, tb=tb_c)
        hf, hb, _, _ = _mlstm(mq, mk, vt, li, lf, cst, mst, tb=tb_x)
        a_x = _attention(aq, ak, av, cak, cav, sink, local=True)
        c_x = _conv(cu, dw, db, lg, lb, pw, tc=tm_x)
        x = _mix_ffn(x, hf, hb, mo, a_x, c_x, hg, wo, part(mx_, 2), part(mx_, 3), part(mx_, 4), part(mx_, 5),
                     gain2, wi, wf, final_gain.reshape(1, D_MODEL) if last else None, tm=tm_x)
        if not last:
            a_c = _attention(caq, cak, cav, cak, cav, sink, local=False)
            c_c = _conv(ccu, dw, db, lg, lb, pw, tc=tm_c)
            ctx = _mix_ffn(ctx, chf, chb, cmo, a_c, c_c, hg, wo, part(mc_, 2), part(mc_, 3), part(mc_, 4),
                           part(mc_, 5), gain2, wi, wf, None, tm=tm_c)
    return x
```

```python
import functools

import numpy as np
import jax
import jax.numpy as jnp
from jax import lax
from jax.experimental import pallas as pl
from jax.experimental.pallas import tpu as pltpu

F32 = jnp.float32
BF16 = jnp.bfloat16

D_MODEL = 1024
DEPTH = 2
GRID_W = 64
HEAD_DIM = 64
M_WIDTH = D_MODEL // 4
M_HEADS = M_WIDTH // HEAD_DIM
N_DIR = 2
A_WIDTH = D_MODEL // 2
A_HEADS = A_WIDTH // HEAD_DIM
A_KV_HEADS = A_HEADS // 4
A_GROUP = A_HEADS // A_KV_HEADS
KV_WIDTH = A_KV_HEADS * HEAD_DIM
A_HEAD_ORDER = np.arange(A_HEADS).reshape(A_KV_HEADS, A_GROUP).T.reshape(-1)
WINDOW = 128
ROPE_THETA = 10000.0
C_WIDTH = D_MODEL // 4
CONV_K = 31
CONV_HALO = 16
FFN_HIDDEN = ((8 * D_MODEL + 3 * 256 - 1) // (3 * 256)) * 256
IN_SPLITS = (M_WIDTH, M_WIDTH, M_WIDTH, M_WIDTH, N_DIR * 2 * M_HEADS,
             A_WIDTH, KV_WIDTH, KV_WIDTH, 2 * C_WIDTH)
EPS = 1e-6
NEG_INF = -1e30

LANES = 128
SUBLANES = 8
VMEM_LIMIT = 52 * 1024 * 1024

C_MQ = 0
C_MK = C_MQ + M_WIDTH
C_MO = C_MK + M_WIDTH
C_AQ = C_MO + M_WIDTH
C_AK = C_AQ + A_WIDTH
C_AV = C_AK + KV_WIDTH
C_CU = C_AV + KV_WIDTH
N_COLS = C_CU + 2 * C_WIDTH
GATE_ROWS = N_DIR * SUBLANES
R_LI = M_WIDTH
R_LF = R_LI + GATE_ROWS
NT_ROWS = R_LF + GATE_ROWS
M_PAIRS = M_HEADS // 2
PAIR_W = 2 * HEAD_DIM
STATE_ROWS = PAIR_W + SUBLANES
M_CHUNK = LANES


def _cparams(sem):
    return pltpu.CompilerParams(dimension_semantics=sem, vmem_limit_bytes=VMEM_LIMIT)


def _const_spec(shape):
    nd = len(shape)
    return pl.BlockSpec(shape, lambda *_: (0,) * nd, pipeline_mode=pl.Buffered(1))


def _log_sigmoid(x):
    return jnp.minimum(x, 0.0) - jnp.log1p(jnp.exp(-jnp.abs(x)))


def _sigmoid(x):
    return jax.nn.sigmoid(x)


def _rms(x):
    return x * lax.rsqrt(jnp.mean(x * x, axis=-1, keepdims=True) + EPS)


def _dot(a, b):
    return jnp.dot(a, b, preferred_element_type=F32)


def _dot_nt(a, b):
    return lax.dot_general(a, b, (((1,), (1,)), ((), ())), preferred_element_type=F32)


def _split_bf16(x, parts):
    out = []
    r = x
    for _ in range(parts):
        p = r.astype(BF16)
        out.append(p)
        r = r - p.astype(F32)
    return out


def _rows(x, i, n):
    return jnp.broadcast_to(x[i:i + 1], (n, x.shape[1]))


def _mod_kernel(c_ref, w_ref, b_ref, o_ref):
    cc = c_ref[...]
    s = (cc * _sigmoid(cc)).astype(BF16)
    o_ref[0] = _dot(s, w_ref[0].astype(BF16)) + b_ref[0]


def _modulation(cc, w_mod, b_mod):
    rows = cc.shape[0]
    n = w_mod.shape[-1]
    tn = 1536
    return pl.pallas_call(
        _mod_kernel,
        grid=(DEPTH, n // tn),
        in_specs=[pl.BlockSpec((rows, D_MODEL), lambda l, j: (0, 0)),
                  pl.BlockSpec((1, D_MODEL, tn), lambda l, j: (l, 0, j)),
                  pl.BlockSpec((1, 1, tn), lambda l, j: (l, 0, j))],
        out_specs=pl.BlockSpec((1, rows, tn), lambda l, j: (l, 0, j)),
        out_shape=jax.ShapeDtypeStruct((DEPTH, rows, n), F32),
        compiler_params=_cparams(("arbitrary", "arbitrary")),
        name="modulation",
    )(cc, w_mod, b_mod.reshape(DEPTH, 1, n))


def _inproj_kernel(x_ref, gain_ref, shift_ref, scale_ref, w_ref, wt_ref, brow_ref, cos_ref, sin_ref,
                   mq_ref, mk_ref, mo_ref, vt_ref, li_ref, lf_ref, aq_ref, ak_ref, av_ref, cu_ref, *, rope):
    xf = x_ref[0]
    h = _rms(xf) * gain_ref[...]
    h = h * (1.0 + scale_ref[0]) + shift_ref[0]
    hb = h.astype(BF16)
    p = _dot(hb, w_ref[...])
    pt = _dot_nt(wt_ref[...], hb)
    mq_ref[0] = p[:, C_MQ:C_MK].astype(BF16)
    mk_ref[0] = p[:, C_MK:C_MO].astype(BF16)
    mo_ref[0] = p[:, C_MO:C_AQ]
    av_ref[0] = p[:, C_AV:C_CU].astype(BF16)
    cu_ref[0] = p[:, C_CU:N_COLS]
    vt_ref[0] = pt[0:M_WIDTH].astype(BF16)
    g = pt[R_LI:NT_ROWS] + brow_ref[...]
    li_ref[0] = g[0:GATE_ROWS]
    lf = g[GATE_ROWS:2 * GATE_ROWS]
    r = lax.broadcasted_iota(jnp.int32, lf.shape, 0)
    lf_ref[0] = jnp.where(r % SUBLANES < M_HEADS, _log_sigmoid(lf), 0.0)
    if rope:
        cos = cos_ref[...]
        sin = sin_ref[...]
        lane = lax.broadcasted_iota(jnp.int32, cos.shape, 1)
        first = (lane % 32) < 16

        def rot(z):
            sw = jnp.where(first, pltpu.roll(z, LANES - 16, 1), pltpu.roll(z, 16, 1))
            return z * cos + sw * sin

        for j in range(A_WIDTH // LANES):
            aq_ref[0, :, j * LANES:(j + 1) * LANES] = rot(
                p[:, C_AQ + j * LANES:C_AQ + (j + 1) * LANES]).astype(BF16)
        ak_ref[0] = rot(p[:, C_AK:C_AV]).astype(BF16)
    else:
        aq_ref[0] = p[:, C_AQ:C_AK].astype(BF16)
        ak_ref[0] = p[:, C_AK:C_AV].astype(BF16)


def _inproj(x, gain, shift, scale, w, wt, brow, cos, sin, *, rope, tm):
    b, s, _ = x.shape
    tok = lambda width: pl.BlockSpec((1, tm, width), lambda bi, i: (bi, i, 0))
    vec = pl.BlockSpec((1, 1, D_MODEL), lambda bi, i: (bi, 0, 0))
    tab = pl.BlockSpec((tm, LANES), lambda bi, i: (i, 0))
    tr = lambda rows: pl.BlockSpec((1, rows, tm), lambda bi, i: (bi, 0, i))
    tok_shape = lambda width, dt: jax.ShapeDtypeStruct((b, s, width), dt)
    tr_shape = lambda rows, dt: jax.ShapeDtypeStruct((b, rows, s), dt)
    out_specs = [tok(M_WIDTH), tok(M_WIDTH), tok(M_WIDTH), tr(M_WIDTH), tr(GATE_ROWS), tr(GATE_ROWS),
                 tok(A_WIDTH), tok(KV_WIDTH), tok(KV_WIDTH), tok(2 * C_WIDTH)]
    out_shape = [tok_shape(M_WIDTH, BF16), tok_shape(M_WIDTH, BF16), tok_shape(M_WIDTH, F32), tr_shape(M_WIDTH, BF16),
                 tr_shape(GATE_ROWS, F32), tr_shape(GATE_ROWS, F32),
                 tok_shape(A_WIDTH, BF16), tok_shape(KV_WIDTH, BF16), tok_shape(KV_WIDTH, BF16),
                 tok_shape(2 * C_WIDTH, F32)]
    return pl.pallas_call(
        functools.partial(_inproj_kernel, rope=rope),
        grid=(b, s // tm),
        in_specs=[tok(D_MODEL), _const_spec((1, D_MODEL)), vec, vec,
                  _const_spec((D_MODEL, N_COLS)), _const_spec((NT_ROWS, D_MODEL)),
                  _const_spec((2 * GATE_ROWS, 1)), tab, tab],
        out_specs=out_specs,
        out_shape=out_shape,
        compiler_params=_cparams(("parallel", "parallel")),
        name="inproj_rope" if rope else "inproj_ctx",
    )(x, gain, shift, scale, w, wt, brow, cos, sin)


def _mlstm_kernel(qf_ref, kf_ref, vtf_ref, lif_ref, lff_ref, qb_ref, kb_ref, vtb_ref, lib_ref, lfb_ref,
                  c0_ref, m0_ref, hf_ref, hb_ref, cst_ref, mst_ref, *, nsub):
    t = M_CHUNK

    @pl.when(pl.program_id(1) == 0)
    def _():
        cst_ref[...] = c0_ref[...]
        mst_ref[...] = m0_ref[...]

    key = lax.broadcasted_iota(jnp.int32, (t, t), 0)
    qry = lax.broadcasted_iota(jnp.int32, (t, t), 1)
    key2 = lax.broadcasted_iota(jnp.int32, (2 * t, t), 0) % t
    qry2 = lax.broadcasted_iota(jnp.int32, (2 * t, t), 1)
    first_head_lanes = lax.broadcasted_iota(jnp.int32, (t, PAIR_W), 1) < HEAD_DIM
    first_head_rows = lax.broadcasted_iota(jnp.int32, (PAIR_W, t), 0) < HEAD_DIM
    den_r = lax.broadcasted_iota(jnp.int32, (SUBLANES, 2 * t), 0)
    den_c = lax.broadcasted_iota(jnp.int32, (SUBLANES, 2 * t), 1) // t
    st_r = lax.broadcasted_iota(jnp.int32, (STATE_ROWS, PAIR_W), 0)
    st_c = lax.broadcasted_iota(jnp.int32, (STATE_ROWS, PAIR_W), 1) // HEAD_DIM
    dirs = ((qf_ref, kf_ref, vtf_ref, lif_ref, lff_ref, hf_ref),
            (qb_ref, kb_ref, vtb_ref, lib_ref, lfb_ref, hb_ref))
    chains = [(d, p) for d in range(N_DIR) for p in range(M_PAIRS)]
    visible = [key <= qry, key >= qry]
    visible2 = [key2 <= qry2, key2 >= qry2]
    ones_st = [jnp.where(v, 1.0, 0.0).astype(BF16) for v in visible]
    state = {dp: cst_ref[0, dp[0] * M_PAIRS + dp[1]] for dp in chains}
    m_state = [mst_ref[0, d] for d in range(N_DIR)]

    gates, pairs = {}, {}
    for step in range(nsub):
        for d in range(N_DIR):
            rev = d == 1
            j = nsub - 1 - step if rev else step
            tok = slice(j * t, (j + 1) * t)
            gate_rows = slice(d * SUBLANES, (d + 1) * SUBLANES)
            li8 = dirs[d][3][0, gate_rows, tok]
            lf8 = dirs[d][4][0, gate_rows, tok]
            a8 = sum(_dot(part, ones_st[d]) for part in _split_bf16(lf8, 3))
            b8 = li8 - a8
            a_end = jnp.broadcast_to(a8[:, 0:1] if rev else a8[:, t - 1:t], (SUBLANES, t))
            g8 = a_end + b8
            g_max = jnp.broadcast_to(jnp.max(g8, axis=1, keepdims=True), (SUBLANES, t))
            b_col = jnp.transpose(jnp.concatenate([b8, jnp.zeros((t - SUBLANES, t), F32)], axis=0))
            gates[step, d] = (a8, a_end, g8, g_max)
            for p in range(M_PAIRS):
                h0, h1 = 2 * p, 2 * p + 1
                lanes = slice(p * PAIR_W, (p + 1) * PAIR_W)
                q01 = dirs[d][0][0, tok, lanes].astype(BF16)
                k01 = dirs[d][1][0, tok, lanes]
                vt01 = dirs[d][2][0, lanes, tok]
                kbd = jnp.concatenate([jnp.where(first_head_lanes, k01, 0.0),
                                       jnp.where(first_head_lanes, 0.0, k01)], axis=0).astype(BF16)
                s2 = _dot_nt(kbd, q01)
                bc = jnp.concatenate([jnp.broadcast_to(b_col[:, h0:h0 + 1], (t, t)),
                                      jnp.broadcast_to(b_col[:, h1:h1 + 1], (t, t))], axis=0)
                bc = jnp.where(visible2[d], bc, -jnp.inf)
                cm = [jnp.max(bc[i * t:(i + 1) * t], axis=0, keepdims=True) for i in range(2)]
                vbd =jnp.concatenate([jnp.where(first_head_rows, vt01, 0.0),
                                       jnp.where(first_head_rows, 0.0, vt01)], axis=1)
                den_rows = jnp.where(((den_r == h0) & (den_c == 0)) | ((den_r == h1) & (den_c == 1)), 1.0, 0.0)
                lhs = jnp.concatenate([vbd, den_rows], axis=0).astype(BF16)
                pairs[step, d, p] = (tok, lanes, q01, k01.astype(BF16), vt01, s2, bc, cm, lhs)

    row8 = lax.broadcasted_iota(jnp.int32, (SUBLANES, t), 0)
    for step in range(nsub):
        small = {}
        for d in range(N_DIR):
            a8, a_end, g8, g_max = gates[step, d]
            m_prev = m_state[d]
            m_new = jnp.maximum(a_end + m_prev, g_max)
            decay8 = jnp.exp(a_end + m_prev - m_new)
            wg8 = jnp.exp(g8 - m_new)
            m_state[d] = m_new
            small[d] = (a8, m_prev, decay8, wg8)
        weighted = {}
        for d, p in chains:
            tok, lanes, q01, k01b, vt01, s2, bc, cm, lhs = pairs[step, d, p]
            a8, m_prev = small[d][0:2]
            mx = [jnp.maximum(cm[i], m_prev[2 * p + i:2 * p + i + 1]) for i in range(2)]
            w = [jnp.exp(m_prev[2 * p + i:2 * p + i + 1] - mx[i]) for i in range(2)]
            e = [jnp.exp(-(a8[2 * p + i:2 * p + i + 1] + mx[i])) for i in range(2)]
            mr = jnp.concatenate([jnp.broadcast_to(mx[0], (t, t)), jnp.broadcast_to(mx[1], (t, t))], axis=0)
            weighted[d, p] = ((s2 * jnp.exp(bc - mr)).astype(BF16), w, e)
        for d, p in chains:
            tok, lanes, q01, k01b, vt01, s2, bc, cm, lhs = pairs[step, d, p]
            a8, m_prev, decay8, wg8 = small[d]
            sp, w, e = weighted[d, p]
            h0, h1 = 2 * p, 2 * p + 1
            st = state[d, p]
            intra = _dot(lhs, sp)
            inter = _dot_nt(st.astype(BF16), q01)
            w8 = jnp.where(row8 == h0, w[0], jnp.where(row8 == h1, w[1], 0.0))
            w_rows = jnp.concatenate([jnp.broadcast_to(w[0], (HEAD_DIM, t)), jnp.broadcast_to(w[1], (HEAD_DIM, t)),
                                      w8], axis=0)
            tot = w_rows * inter + intra
            den = jnp.concatenate([_rows(tot, PAIR_W + h0, HEAD_DIM), _rows(tot, PAIR_W + h1, HEAD_DIM)], axis=0)
            e_rows = jnp.concatenate([jnp.broadcast_to(e[0], (HEAD_DIM, t)), jnp.broadcast_to(e[1], (HEAD_DIM, t))],
                                     axis=0)
            h_t = tot[0:PAIR_W] / jnp.maximum(jnp.abs(den), e_rows)
            dirs[d][5][0, tok, lanes] = jnp.transpose(h_t)
            wg_rows = jnp.concatenate([_rows(wg8, h0, HEAD_DIM), _rows(wg8, h1, HEAD_DIM)], axis=0)
            upd = _dot(jnp.concatenate([vt01 * wg_rows, wg8], axis=0).astype(BF16), k01b)
            keep = (st_r // HEAD_DIM == st_c) | (st_r - PAIR_W == st_c + h0)
            dec = jnp.concatenate([_rows(decay8, h0, HEAD_DIM), _rows(decay8, h1, HEAD_DIM), decay8], axis=0)
            state[d, p] = dec * st + jnp.where(keep, upd, 0.0)

    for d, p in chains:
        cst_ref[0, d * M_PAIRS + p] = state[d, p]
    for d in range(N_DIR):
        mst_ref[0, d] = m_state[d]


def _mlstm(mq, mk, vt, li, lf, c0, m0, *, tb):
    b, s, _ = mq.shape
    nc = s // tb
    fwd = lambda width: pl.BlockSpec((1, tb, width), lambda bi, c: (bi, c, 0))
    bwd = lambda width: pl.BlockSpec((1, tb, width), lambda bi, c: (bi, nc - 1 - c, 0))
    fwd_t = lambda rows: pl.BlockSpec((1, rows, tb), lambda bi, c: (bi, 0, c))
    bwd_t = lambda rows: pl.BlockSpec((1, rows, tb), lambda bi, c: (bi, 0, nc - 1 - c))
    st_c = pl.BlockSpec((1, N_DIR * M_PAIRS, STATE_ROWS, PAIR_W), lambda bi, c: (bi, 0, 0, 0))
    st_m = pl.BlockSpec((1, N_DIR, SUBLANES, LANES), lambda bi, c: (bi, 0, 0, 0))
    return pl.pallas_call(
        functools.partial(_mlstm_kernel, nsub=tb // M_CHUNK),
        grid=(b, nc),
        in_specs=[fwd(M_WIDTH), fwd(M_WIDTH), fwd_t(M_WIDTH), fwd_t(GATE_ROWS), fwd_t(GATE_ROWS),
                  bwd(M_WIDTH), bwd(M_WIDTH), bwd_t(M_WIDTH), bwd_t(GATE_ROWS), bwd_t(GATE_ROWS),
                  st_c, st_m],
        out_specs=[fwd(M_WIDTH), bwd(M_WIDTH), st_c, st_m],
        out_shape=[jax.ShapeDtypeStruct((b, s, M_WIDTH), F32), jax.ShapeDtypeStruct((b, s, M_WIDTH), F32),
                   jax.ShapeDtypeStruct(c0.shape, F32), jax.ShapeDtypeStruct(m0.shape, F32)],
        compiler_params=_cparams(("parallel", "arbitrary")),
        name="mlstm_scan",
    )(mq, mk, vt, li, lf, mq, mk, vt, li, lf, c0, m0)


def _attn_kernel(*refs, seq, nq, local):
    tq = WINDOW
    if local:
        q_ref, kp_ref, kc_ref, kn_ref, vp_ref, vc_ref, vn_ref, kx_ref, vx_ref, sink_ref, o_ref = refs
        k_blocks = [kp_ref[0]] + [kc_ref[0, i * tq:(i + 1) * tq] for i in range(nq)] + [kn_ref[0]]
        v_blocks = [vp_ref[0]] + [vc_ref[0, i * tq:(i + 1) * tq] for i in range(nq)] + [vn_ref[0]]
        start = pl.program_id(1) * (nq * tq)
        row = lax.broadcasted_iota(jnp.int32, (tq, 2 * tq), 0)
        side = lax.broadcasted_iota(jnp.int32, (tq, 2 * tq), 1)
    else:
        q_ref, kx_ref, vx_ref, sink_ref, o_ref = refs
    n_keys = kx_ref.shape[1] + (3 * tq if local else 0)
    kv_of_lane = lax.broadcasted_iota(jnp.int32, (1, KV_WIDTH), 1) // HEAD_DIM
    sel = [jnp.where(kv_of_lane == g, 1.0, 0.0).astype(BF16) for g in range(A_KV_HEADS)]
    first_kv_q =lax.broadcasted_iota(jnp.int32, (A_GROUP * tq, KV_WIDTH), 1) < HEAD_DIM
    scores, sinks, values = [], [], []
    for blk in range(nq):
        if local:
            k_cat = jnp.concatenate([k_blocks[blk], k_blocks[blk + 2], k_blocks[blk + 1], kx_ref[0]], axis=0)
            v_cat = jnp.concatenate([v_blocks[blk], v_blocks[blk + 2], v_blocks[blk + 1], vx_ref[0]], axis=0)
            qpos = start + blk * tq + row
            kpos = start + blk * tq + jnp.where(side < tq, side - tq, side)
            ok = (jnp.abs(qpos - kpos) <= WINDOW) & (kpos >= 0) & (kpos < seq)
            bias = jnp.concatenate([jnp.where(ok, 0.0, NEG_INF)] * A_GROUP, axis=0)
        else:
            k_cat, v_cat = kx_ref[0], vx_ref[0]
        k_bd = jnp.concatenate([k_cat * sel[0], k_cat * sel[1]], axis=0)
        v_bd = jnp.concatenate(
            [jnp.concatenate([v_cat * sel[g], jnp.broadcast_to(sel[g], v_cat.shape)], axis=1)
             for g in range(A_KV_HEADS)], axis=0)
        qs = jnp.concatenate([q_ref[0, blk * tq:(blk + 1) * tq, j * KV_WIDTH:(j + 1) * KV_WIDTH]
                              for j in range(A_GROUP)], axis=0)
        s = _dot_nt(qs, k_bd)
        if local:
            s = jnp.concatenate([s[:, :2 * tq] + bias, s[:, 2 * tq:n_keys],
                                 s[:, n_keys:n_keys + 2 * tq] + bias, s[:, n_keys + 2 * tq:]], axis=1)
        scores.append(s)
        values.append(v_bd)
        sinks.append([jnp.concatenate(
            [jnp.broadcast_to(sink_ref[0:1, g * A_GROUP + j:g * A_GROUP + j + 1], (tq, KV_WIDTH))
             for j in range(A_GROUP)], axis=0) for g in range(A_KV_HEADS)])
    maxes = [[jnp.maximum(jnp.broadcast_to(jnp.max(s[:, g * n_keys:(g + 1) * n_keys], axis=1, keepdims=True),
                                           sk[g].shape), sk[g]) for g in range(A_KV_HEADS)]
             for s, sk in zip(scores, sinks)]
    probs = [jnp.concatenate([jnp.exp(s[:, i:i + LANES] - m[i // n_keys]).astype(BF16)
                              for i in range(0, 2 * n_keys, LANES)], axis=1) for s, m in zip(scores, maxes)]
    outs = [_dot(p, v_bd) for p, v_bd in zip(probs, values)]
    for blk, (ol, sk, m) in enumerate(zip(outs, sinks, maxes)):
        sink_term = jnp.exp(jnp.where(first_kv_q, sk[0] - m[0], sk[1] - m[1]))
        o = (ol[:, :KV_WIDTH] / (ol[:, KV_WIDTH:] + sink_term)).astype(o_ref.dtype)
        for j in range(A_GROUP):
            o_ref[0, blk * tq:(blk + 1) * tq, j * KV_WIDTH:(j + 1) * KV_WIDTH] = o[j * tq:(j + 1) * tq]


def _attention(aq, ak, av, kx, vx, sink, *, local):
    b, s, _ = aq.shape
    n_ctx = kx.shape[1]
    tq = WINDOW
    nb = s // tq
    nq = min(4, nb)
    qs = pl.BlockSpec((1, nq * tq, A_WIDTH), lambda bi, i: (bi, i, 0))
    cx = pl.BlockSpec((1, n_ctx, KV_WIDTH), lambda bi, i: (bi, 0, 0))
    sk = pl.BlockSpec((1, A_HEADS), lambda bi, i: (0, 0))
    if local:
        prv = pl.BlockSpec((1, tq, KV_WIDTH), lambda bi, i: (bi, jnp.maximum(nq * i - 1, 0), 0))
        cur = pl.BlockSpec((1, nq * tq, KV_WIDTH), lambda bi, i: (bi, i, 0))
        nxt = pl.BlockSpec((1, tq, KV_WIDTH), lambda bi, i: (bi, jnp.minimum(nq * (i + 1), nb - 1), 0))
        in_specs = [qs, prv, cur, nxt, prv, cur, nxt, cx, cx, sk]
        args = (aq, ak, ak, ak, av, av, av, kx, vx, sink)
    else:
        in_specs = [qs, cx, cx, sk]
        args = (aq, kx, vx, sink)
    return pl.pallas_call(
        functools.partial(_attn_kernel, seq=s, nq=nq, local=local),
        grid=(b, nb // nq),
        in_specs=in_specs,
        out_specs=qs,
        out_shape=jax.ShapeDtypeStruct((b, s, A_WIDTH), BF16),
        compiler_params=_cparams(("parallel", "parallel")),
        name="window_attn" if local else "ctx_attn",
    )(*args)


def _conv_kernel(up_ref, uc_ref, un_ref, dw_ref, db_ref, lg_ref, lb_ref, pw_ref, o_ref, ybuf, *, tc, sub):
    i = pl.program_id(1)
    nblk = pl.num_programs(1)
    span = tc + 2 * CONV_HALO - SUBLANES

    def glu(u):
        return u[:, :C_WIDTH] * _sigmoid(u[:, C_WIDTH:])

    ybuf[0, 0:CONV_HALO] = jnp.where(i > 0, glu(up_ref[0]), 0.0)
    ybuf[0, CONV_HALO:CONV_HALO + tc] = glu(uc_ref[0])
    ybuf[0, CONV_HALO + tc:2 * CONV_HALO + tc] = jnp.where(i < nblk - 1, glu(un_ref[0]), 0.0)
    for r in range(1, SUBLANES):
        ybuf[r, 0:span] = ybuf[0, r:r + span]
    off = CONV_HALO - CONV_K // 2
    for r0 in range(0, tc, sub):
        acc = jnp.zeros((sub // SUBLANES, SUBLANES, C_WIDTH), F32)
        for k in range(CONV_K):
            r = (off + k) % SUBLANES
            base = r0 + off + k - r
            acc = acc + ybuf[r, base:base + sub, :].reshape(sub // SUBLANES, SUBLANES, C_WIDTH) * dw_ref[k]
        y = acc.reshape(sub, C_WIDTH) + db_ref[...]
        mu = jnp.mean(y, axis=-1, keepdims=True)
        yc = y - mu
        var = jnp.mean(yc * yc, axis=-1, keepdims=True)
        z = yc * lax.rsqrt(var + EPS) * lg_ref[...] + lb_ref[...]
        z = z * _sigmoid(z)
        o_ref[0, r0:r0 + sub, :] = _dot(z.astype(BF16), pw_ref[...]).astype(o_ref.dtype)


def _conv(cu, dw, db, lg, lb, pw, *, tc):
    b, s, _ = cu.shape
    nh = s // CONV_HALO
    r = tc // CONV_HALO
    cur = pl.BlockSpec((1, tc, 2 * C_WIDTH), lambda bi, i: (bi, i, 0))
    prv = pl.BlockSpec((1, CONV_HALO, 2 * C_WIDTH), lambda bi, i: (bi, jnp.maximum(i * r - 1, 0), 0))
    nxt = pl.BlockSpec((1, CONV_HALO, 2 * C_WIDTH), lambda bi, i: (bi, jnp.minimum((i + 1) * r, nh - 1), 0))
    return pl.pallas_call(
        functools.partial(_conv_kernel, tc=tc, sub=64),
        grid=(b, s // tc),
        in_specs=[prv, cur, nxt, _const_spec((CONV_K + 1, SUBLANES, C_WIDTH)), _const_spec((1, C_WIDTH)),
                  _const_spec((1, C_WIDTH)), _const_spec((1, C_WIDTH)), _const_spec((C_WIDTH, C_WIDTH))],
        out_specs=pl.BlockSpec((1, tc, C_WIDTH), lambda bi, i: (bi, i, 0)),
        out_shape=jax.ShapeDtypeStruct((b, s, C_WIDTH), BF16),
        scratch_shapes=[pltpu.VMEM((SUBLANES, tc + 2 * CONV_HALO, C_WIDTH), F32)],
        compiler_params=_cparams(("parallel", "parallel")),
        name="conformer_conv",
    )(cu, cu, cu, dw, db, lg, lb, pw)


def _mix_ffn_kernel(*refs, final, n_chunk):
    (x_ref, hf_ref, hb_ref, mo_ref, a_ref, c_ref, hg_ref, wo_ref, g1_ref, sh2_ref, sc2_ref, g2_ref,
     gain2_ref, wi_ref, wf_ref) = refs[:15]
    fg_ref = refs[15] if final else None
    o_ref = refs[-1]
    hs = hf_ref[0] + hb_ref[0]
    ri = lax.broadcasted_iota(jnp.int32, (M_WIDTH, M_WIDTH), 0) // HEAD_DIM
    ci = lax.broadcasted_iota(jnp.int32, (M_WIDTH, M_WIDTH), 1) // HEAD_DIM
    same_head = jnp.where(ri == ci, 1.0, 0.0).astype(BF16)
    ms = sum(_dot(part, same_head) for part in _split_bf16(hs * hs, 2)) * (1.0 / HEAD_DIM)
    mx = hs * lax.rsqrt(ms + EPS) * hg_ref[...] * _sigmoid(mo_ref[0])
    y = (_dot(mx.astype(BF16), wo_ref[0:M_WIDTH, :])
         + _dot(a_ref[0].astype(BF16), wo_ref[M_WIDTH:M_WIDTH + A_WIDTH, :])
         + _dot(c_ref[0].astype(BF16), wo_ref[M_WIDTH + A_WIDTH:, :]))
    x1 = x_ref[0] + g1_ref[0] * y
    h2 = (_rms(x1) * gain2_ref[...]) * (1.0 + sc2_ref[0]) + sh2_ref[0]
    hb2 = h2.astype(BF16)
    ch = FFN_HIDDEN // n_chunk
    acc = None
    for j in range(n_chunk):
        gate = _dot(hb2, wi_ref[:, j * ch:(j + 1) * ch])
        up = _dot(hb2, wi_ref[:, FFN_HIDDEN + j * ch:FFN_HIDDEN + (j + 1) * ch])
        hj = (gate * _sigmoid(gate) * up).astype(BF16)
        part = _dot(hj, wf_ref[j * ch:(j + 1) * ch, :])
        acc = part if acc is None else acc + part
    x2 = x1 + g2_ref[0] * acc
    if final:
        x2 = _rms(x2) * fg_ref[...]
    o_ref[0] = x2


def _mix_ffn(x, hf, hb, mo, a, c, hg, wo, g1, sh2, sc2, g2, gain2, wi, wf, fg, *, tm):
    b, s, _ = x.shape
    final = fg is not None
    tok = lambda width: pl.BlockSpec((1, tm, width), lambda bi, i: (bi, i, 0))
    vec = pl.BlockSpec((1, 1, D_MODEL), lambda bi, i: (bi, 0, 0))
    in_specs = [tok(D_MODEL), tok(M_WIDTH), tok(M_WIDTH), tok(M_WIDTH), tok(A_WIDTH), tok(C_WIDTH),
                _const_spec((1, M_WIDTH)), _const_spec((D_MODEL, D_MODEL)), vec, vec, vec, vec,
                _const_spec((1, D_MODEL)), _const_spec((D_MODEL, 2 * FFN_HIDDEN)),
                _const_spec((FFN_HIDDEN, D_MODEL))]
    args = [x, hf, hb, mo, a, c, hg, wo, g1, sh2, sc2, g2, gain2, wi, wf]
    if final:
        in_specs.append(_const_spec((1, D_MODEL)))
        args.append(fg)
    return pl.pallas_call(
        functools.partial(_mix_ffn_kernel, final=final, n_chunk=1),
        grid=(b, s // tm),
        in_specs=in_specs,
        out_specs=tok(D_MODEL),
        out_shape=jax.ShapeDtypeStruct((b, s, D_MODEL), F32),
        compiler_params=_cparams(("parallel", "parallel")),
        name="mix_ffn_final" if final else "mix_ffn",
    )(*args)


def _rope_tables(seq):
    half = HEAD_DIM // 4
    freqs = ROPE_THETA ** (-jnp.arange(half, dtype=F32) / half)
    t = jnp.arange(seq)
    sign = jnp.concatenate([-jnp.ones((half,), F32), jnp.ones((half,), F32)])
    cos_parts, sin_parts = [], []
    for pos in (t // GRID_W, t % GRID_W):
        ang = pos.astype(F32)[:, None] * freqs[None, :]
        c, s = jnp.cos(ang), jnp.sin(ang)
        cos_parts.append(jnp.concatenate([c, c], axis=-1))
        sin_parts.append(jnp.concatenate([s, s], axis=-1) * sign)
    cos = jnp.concatenate(cos_parts, axis=-1)
    sin = jnp.concatenate(sin_parts, axis=-1)
    rep = LANES // HEAD_DIM
    return jnp.tile(cos, (1, rep)), jnp.tile(sin, (1, rep))


def _layer_weights(w_in, gate_bias):
    o = np.cumsum((0,) + IN_SPLITS)
    col = lambda i: w_in[:, int(o[i]):int(o[i + 1])]
    scale = HEAD_DIM ** -0.5
    aq = (col(5) * scale).reshape(D_MODEL, A_HEADS, HEAD_DIM)[:, A_HEAD_ORDER].reshape(D_MODEL, A_WIDTH)
    w = jnp.concatenate([col(0), col(1) * scale, col(3), aq, col(6), col(7), col(8)], axis=1).astype(BF16)
    gates = col(4).T.reshape(N_DIR, 2, M_HEADS, D_MODEL)
    pad = ((0, 0), (0, SUBLANES - M_HEADS), (0, 0))
    li_rows = jnp.pad(gates[:, 0], pad).reshape(GATE_ROWS, D_MODEL)
    lf_rows = jnp.pad(gates[:, 1], pad).reshape(GATE_ROWS, D_MODEL)
    wt = jnp.concatenate([col(2).T, li_rows, lf_rows], axis=0).astype(BF16)
    bias = jnp.pad(gate_bias.astype(F32), ((0, 0), (0, 0), (0, SUBLANES - M_HEADS)))
    brow = jnp.concatenate([bias[:, 0].reshape(GATE_ROWS), bias[:, 1].reshape(GATE_ROWS)]).reshape(2 * GATE_ROWS, 1)
    return w, wt, brow


def kernel(x, c, ctx, c_ctx, w_mod, b_mod, norm_gain, w_in, mlstm_gate_bias, mlstm_head_gain, attn_sink,
           conv_dw_w, conv_dw_b, conv_ln_g, conv_ln_b, conv_pw_w, w_out, w_ffn_in, w_ffn_out, final_gain):
    b, seq, _ = x.shape
    n_ctx = ctx.shape[1]
    rows = -(-(b + 1) // SUBLANES) * SUBLANES
    cc = jnp.zeros((rows, D_MODEL), F32).at[:b].set(c).at[b].set(c_ctx)
    mod = _modulation(cc, w_mod, b_mod)
    cos_x, sin_x = _rope_tables(seq)
    cos_c, sin_c = cos_x[:n_ctx], sin_x[:n_ctx]
    zero_c = jnp.zeros((b, N_DIR * M_PAIRS, STATE_ROWS, PAIR_W), F32)
    zero_m = jnp.zeros((b, N_DIR, SUBLANES, LANES), F32)
    tm_x = min(512, seq)
    tm_c = min(256, n_ctx)
    tb_x = min(1024, seq)
    tb_c = min(256, n_ctx)

    for l in range(DEPTH):
        last = l == DEPTH - 1
        mx_ = mod[l, :b].reshape(b, 1, 6 * D_MODEL)
        mc_ = jnp.broadcast_to(mod[l, b].reshape(1, 1, 6 * D_MODEL), (b, 1, 6 * D_MODEL))
        part = lambda m, i: m[:, :, i * D_MODEL:(i + 1) * D_MODEL]
        w, wt, brow = _layer_weights(w_in[l], mlstm_gate_bias[l])
        gain1 = norm_gain[l, 0].reshape(1, D_MODEL)
        gain2 = norm_gain[l, 1].reshape(1, D_MODEL)
        hg = mlstm_head_gain[l].reshape(1, M_WIDTH)
        sink = attn_sink[l].reshape(1, A_HEADS).astype(F32)
        dw = jnp.broadcast_to(jnp.pad(conv_dw_w[l], ((0, 1), (0, 0)))[:, None, :], (CONV_K + 1, SUBLANES, C_WIDTH))
        db = conv_dw_b[l].reshape(1, C_WIDTH)
        lg = conv_ln_g[l].reshape(1, C_WIDTH)
        lb = conv_ln_b[l].reshape(1, C_WIDTH)
        pw = conv_pw_w[l].astype(BF16)
        wo_a = w_out[l, M_WIDTH:M_WIDTH + A_WIDTH].reshape(A_HEADS, HEAD_DIM, D_MODEL)[A_HEAD_ORDER, :, :]
        wo = jnp.concatenate([w_out[l, :M_WIDTH], wo_a.reshape(A_WIDTH, D_MODEL), w_out[l, M_WIDTH + A_WIDTH:]],
                             axis=0).astype(BF16)
        wi = w_ffn_in[l].astype(BF16)
        wf = w_ffn_out[l].astype(BF16)

        px = _inproj(x, gain1, part(mx_, 0), part(mx_, 1), w, wt, brow, cos_x, sin_x, rope=True, tm=tm_x)
        pc = _inproj(ctx, gain1, part(mc_, 0), part(mc_, 1), w, wt, brow, cos_c, sin_c, rope=False, tm=tm_c)
        mq, mk, mo, vt, li, lf, aq, ak, av, cu = px
        cmq, cmk, cmo, cvt, cli, clf, caq, cak, cav, ccu = pc

        chf, chb, cst, mst = _mlstm(cmq, cmk, cvt, cli, clf, zero_c, zero_m, tb=tb_c)
        hf, hb, _, _ = _mlstm(mq, mk, vt, li, lf, cst, mst, tb=tb_x)
        a_x = _attention(aq, ak, av, cak, cav, sink, local=True)
        c_x = _conv(cu, dw, db, lg, lb, pw, tc=tm_x)
        x = _mix_ffn(x, hf, hb, mo, a_x, c_x, hg, wo, part(mx_, 2), part(mx_, 3), part(mx_, 4), part(mx_, 5),
                     gain2, wi, wf, final_gain.reshape(1, D_MODEL) if last else None, tm=tm_x)
        if not last:
            a_c = _attention(caq, cak, cav, cak, cav, sink, local=False)
            c_c = _conv(ccu, dw, db, lg, lb, pw, tc=tm_c)
            ctx = _mix_ffn(ctx, chf, chb, cmo, a_c, c_c, hg, wo, part(mc_, 2), part(mc_, 3), part(mc_, 4),
                           part(mc_, 5), gain2, wi, wf, None, tm=tm_c)
    return x
```

```python
import functools

import numpy as np
import jax
import jax.numpy as jnp
from jax import lax
from jax.experimental import pallas as pl
from jax.experimental.pallas import tpu as pltpu

F32 = jnp.float32
BF16 = jnp.bfloat16

D_MODEL = 1024
DEPTH = 2
GRID_W = 64
HEAD_DIM = 64
M_WIDTH = D_MODEL // 4
M_HEADS = M_WIDTH // HEAD_DIM
N_DIR = 2
A_WIDTH = D_MODEL // 2
A_HEADS = A_WIDTH // HEAD_DIM
A_KV_HEADS = A_HEADS // 4
A_GROUP = A_HEADS // A_KV_HEADS
KV_WIDTH = A_KV_HEADS * HEAD_DIM
A_HEAD_ORDER = np.arange(A_HEADS).reshape(A_KV_HEADS, A_GROUP).T.reshape(-1)
WINDOW = 128
ROPE_THETA = 10000.0
C_WIDTH = D_MODEL // 4
CONV_K = 31
CONV_HALO = 16
FFN_CHUNKS = 11
CONV_PIECES = 8
FFN_HIDDEN = ((8 * D_MODEL + 3 * 256 - 1) // (3 * 256)) * 256
IN_SPLITS = (M_WIDTH, M_WIDTH, M_WIDTH, M_WIDTH, N_DIR * 2 * M_HEADS,
             A_WIDTH, KV_WIDTH, KV_WIDTH, 2 * C_WIDTH)
EPS = 1e-6
NEG_INF = -1e30

LANES = 128
SUBLANES = 8
VMEM_LIMIT = 52 * 1024 * 1024

C_MQ = 0
C_MK = C_MQ + M_WIDTH
C_MO = C_MK + M_WIDTH
C_AQ = C_MO + M_WIDTH
C_AK = C_AQ + A_WIDTH
C_AV = C_AK + KV_WIDTH
C_CU = C_AV + KV_WIDTH
N_COLS = C_CU + 2 * C_WIDTH
GATE_ROWS = N_DIR * SUBLANES
R_LI = M_WIDTH
R_LF = R_LI + GATE_ROWS
NT_ROWS = R_LF + GATE_ROWS
M_PAIRS = M_HEADS // 2
PAIR_W = 2 * HEAD_DIM
STATE_ROWS = PAIR_W + SUBLANES
M_CHUNK = LANES


def _cparams(sem):
    return pltpu.CompilerParams(dimension_semantics=sem, vmem_limit_bytes=VMEM_LIMIT)


def _const_spec(shape):
    nd = len(shape)
    return pl.BlockSpec(shape, lambda *_: (0,) * nd, pipeline_mode=pl.Buffered(1))


def _log_sigmoid(x):
    return jnp.minimum(x, 0.0) - jnp.log1p(jnp.exp(-jnp.abs(x)))


def _sigmoid(x):
    return jax.nn.sigmoid(x)


def _rms(x):
    return x * lax.rsqrt(jnp.mean(x * x, axis=-1, keepdims=True) + EPS)


def _dot(a, b):
    return jnp.dot(a, b, preferred_element_type=F32)


def _dot_nt(a, b):
    return lax.dot_general(a, b, (((1,), (1,)), ((), ())), preferred_element_type=F32)


def _split_bf16(x, parts):
    out = []
    r = x
    for _ in range(parts):
        p = r.astype(BF16)
        out.append(p)
        r = r - p.astype(F32)
    return out


def _rows(x, i, n):
    return jnp.broadcast_to(x[i:i + 1], (n, x.shape[1]))


def _mod_kernel(c_ref, w_ref, b_ref, o_ref):
    cc = c_ref[...]
    s = (cc * _sigmoid(cc)).astype(BF16)
    o_ref[0] = _dot(s, w_ref[0].astype(BF16)) + b_ref[0]


def _modulation(cc, w_mod, b_mod):
    rows = cc.shape[0]
    n = w_mod.shape[-1]
    tn = 1536
    return pl.pallas_call(
        _mod_kernel,
        grid=(DEPTH, n // tn),
        in_specs=[pl.BlockSpec((rows, D_MODEL), lambda l, j: (0, 0)),
                  pl.BlockSpec((1, D_MODEL, tn), lambda l, j: (l, 0, j)),
                  pl.BlockSpec((1, 1, tn), lambda l, j: (l, 0, j))],
        out_specs=pl.BlockSpec((1, rows, tn), lambda l, j: (l, 0, j)),
        out_shape=jax.ShapeDtypeStruct((DEPTH, rows, n), F32),
        compiler_params=_cparams(("arbitrary", "arbitrary")),
        name="modulation",
    )(cc, w_mod, b_mod.reshape(DEPTH, 1, n))


def _inproj_kernel(x_ref, gain_ref, shift_ref, scale_ref, w_ref, wt_ref, brow_ref, cos_ref, sin_ref,
                   mq_ref, mk_ref, mo_ref, vt_ref, li_ref, lf_ref, aq_ref, ak_ref, av_ref, cu_ref, *, rope):
    xf = x_ref[0]
    h = _rms(xf) * gain_ref[...]
    h = h * (1.0 + scale_ref[0]) + shift_ref[0]
    hb = h.astype(BF16)
    p = _dot(hb, w_ref[...])
    pt = _dot_nt(wt_ref[...], hb)
    mq_ref[0] = p[:, C_MQ:C_MK].astype(BF16)
    mk_ref[0] = p[:, C_MK:C_MO].astype(BF16)
    mo_ref[0] = p[:, C_MO:C_AQ]
    av_ref[0] = p[:, C_AV:C_CU].astype(BF16)
    cu_ref[0] = p[:, C_CU:N_COLS]
    vt_ref[0] = pt[0:M_WIDTH].astype(BF16)
    g = pt[R_LI:NT_ROWS] + brow_ref[...]
    li_ref[0] = g[0:GATE_ROWS]
    lf = g[GATE_ROWS:2 * GATE_ROWS]
    r = lax.broadcasted_iota(jnp.int32, lf.shape, 0)
    lf_ref[0] = jnp.where(r % SUBLANES < M_HEADS, _log_sigmoid(lf), 0.0)
    if rope:
        cos = cos_ref[...]
        sin = sin_ref[...]
        lane = lax.broadcasted_iota(jnp.int32, cos.shape, 1)
        first = (lane % 32) < 16

        def rot(z):
            sw = jnp.where(first, pltpu.roll(z, LANES - 16, 1), pltpu.roll(z, 16, 1))
            return z * cos + sw * sin

        for j in range(A_WIDTH // LANES):
            aq_ref[0, :, j * LANES:(j + 1) * LANES] = rot(
                p[:, C_AQ + j * LANES:C_AQ + (j + 1) * LANES]).astype(BF16)
        ak_ref[0] = rot(p[:, C_AK:C_AV]).astype(BF16)
    else:
        aq_ref[0] = p[:, C_AQ:C_AK].astype(BF16)
        ak_ref[0] = p[:, C_AK:C_AV].astype(BF16)


def _inproj(x, gain, shift, scale, w, wt, brow, cos, sin, *, rope, tm):
    b, s, _ = x.shape
    tok = lambda width: pl.BlockSpec((1, tm, width), lambda bi, i: (bi, i, 0))
    vec = pl.BlockSpec((1, 1, D_MODEL), lambda bi, i: (bi, 0, 0))
    tab = pl.BlockSpec((tm, LANES), lambda bi, i: (i, 0))
    tr = lambda rows: pl.BlockSpec((1, rows, tm), lambda bi, i: (bi, 0, i))
    tok_shape = lambda width, dt: jax.ShapeDtypeStruct((b, s, width), dt)
    tr_shape = lambda rows, dt: jax.ShapeDtypeStruct((b, rows, s), dt)
    out_specs = [tok(M_WIDTH), tok(M_WIDTH), tok(M_WIDTH), tr(M_WIDTH), tr(GATE_ROWS), tr(GATE_ROWS),
                 tok(A_WIDTH), tok(KV_WIDTH), tok(KV_WIDTH), tok(2 * C_WIDTH)]
    out_shape = [tok_shape(M_WIDTH, BF16), tok_shape(M_WIDTH, BF16), tok_shape(M_WIDTH, F32), tr_shape(M_WIDTH, BF16),
                 tr_shape(GATE_ROWS, F32), tr_shape(GATE_ROWS, F32),
                 tok_shape(A_WIDTH, BF16), tok_shape(KV_WIDTH, BF16), tok_shape(KV_WIDTH, BF16),
                 tok_shape(2 * C_WIDTH, F32)]
    return pl.pallas_call(
        functools.partial(_inproj_kernel, rope=rope),
        grid=(b, s // tm),
        in_specs=[tok(D_MODEL), _const_spec((1, D_MODEL)), vec, vec,
                  _const_spec((D_MODEL, N_COLS)), _const_spec((NT_ROWS, D_MODEL)),
                  _const_spec((2 * GATE_ROWS, 1)), tab, tab],
        out_specs=out_specs,
        out_shape=out_shape,
        compiler_params=_cparams(("parallel", "parallel")),
        name="inproj_rope" if rope else "inproj_ctx",
    )(x, gain, shift, scale, w, wt, brow, cos, sin)


def _mlstm_kernel(qf_ref, kf_ref, vtf_ref, lif_ref, lff_ref, qb_ref, kb_ref, vtb_ref, lib_ref, lfb_ref,
                  c0_ref, m0_ref, hf_ref, hb_ref, cst_ref, mst_ref, *, nsub):
    t = M_CHUNK

    @pl.when(pl.program_id(1) == 0)
    def _():
        cst_ref[...] = c0_ref[...]
        mst_ref[...] = m0_ref[...]

    key = lax.broadcasted_iota(jnp.int32, (t, t), 0)
    qry = lax.broadcasted_iota(jnp.int32, (t, t), 1)
    key2 = lax.broadcasted_iota(jnp.int32, (2 * t, t), 0) % t
    qry2 = lax.broadcasted_iota(jnp.int32, (2 * t, t), 1)
    first_head_lanes = lax.broadcasted_iota(jnp.int32, (t, PAIR_W), 1) < HEAD_DIM
    first_head_rows = lax.broadcasted_iota(jnp.int32, (PAIR_W, t), 0) < HEAD_DIM
    den_r = lax.broadcasted_iota(jnp.int32, (SUBLANES, 2 * t), 0)
    den_c = lax.broadcasted_iota(jnp.int32, (SUBLANES, 2 * t), 1) // t
    st_r = lax.broadcasted_iota(jnp.int32, (STATE_ROWS, PAIR_W), 0)
    st_c = lax.broadcasted_iota(jnp.int32, (STATE_ROWS, PAIR_W), 1) // HEAD_DIM
    dirs = ((qf_ref, kf_ref, vtf_ref, lif_ref, lff_ref, hf_ref),
            (qb_ref, kb_ref, vtb_ref, lib_ref, lfb_ref, hb_ref))
    chains = [(d, p) for d in range(N_DIR) for p in range(M_PAIRS)]
    visible = [key <= qry, key >= qry]
    visible2 = [key2 <= qry2, key2 >= qry2]
    ones_st = [jnp.where(v, 1.0, 0.0).astype(BF16) for v in visible]
    state = {dp: cst_ref[0, dp[0] * M_PAIRS + dp[1]] for dp in chains}
    m_state = [mst_ref[0, d] for d in range(N_DIR)]

    gates, pairs = {}, {}
    for step in range(nsub):
        for d in range(N_DIR):
            rev = d == 1
            j = nsub - 1 - step if rev else step
            tok = slice(j * t, (j + 1) * t)
            gate_rows = slice(d * SUBLANES, (d + 1) * SUBLANES)
            li8 = dirs[d][3][0, gate_rows, tok]
            lf8 = dirs[d][4][0, gate_rows, tok]
            a8 = sum(_dot(part, ones_st[d]) for part in _split_bf16(lf8, 3))
            b8 = li8 - a8
            a_end = jnp.broadcast_to(a8[:, 0:1] if rev else a8[:, t - 1:t], (SUBLANES, t))
            g8 = a_end + b8
            g_max = jnp.broadcast_to(jnp.max(g8, axis=1, keepdims=True), (SUBLANES, t))
            b_col = jnp.transpose(jnp.concatenate([b8, jnp.zeros((t - SUBLANES, t), F32)], axis=0))
            gates[step, d] = (a8, a_end, g8, g_max)
            for p in range(M_PAIRS):
                h0, h1 = 2 * p, 2 * p + 1
                lanes = slice(p * PAIR_W, (p + 1) * PAIR_W)
                q01 = dirs[d][0][0, tok, lanes].astype(BF16)
                k01 = dirs[d][1][0, tok, lanes]
                vt01 = dirs[d][2][0, lanes, tok]
                kbd = jnp.concatenate([jnp.where(first_head_lanes, k01, 0.0),
                                       jnp.where(first_head_lanes, 0.0, k01)], axis=0).astype(BF16)
                s2 = _dot_nt(kbd, q01)
                bc = jnp.concatenate([jnp.broadcast_to(b_col[:, h0:h0 + 1], (t, t)),
                                      jnp.broadcast_to(b_col[:, h1:h1 + 1], (t, t))], axis=0)
                bc = jnp.where(visible2[d], bc, -jnp.inf)
                cm = [jnp.max(bc[i * t:(i + 1) * t], axis=0, keepdims=True) for i in range(2)]
                vbd =jnp.concatenate([jnp.where(first_head_rows, vt01, 0.0),
                                       jnp.where(first_head_rows, 0.0, vt01)], axis=1)
                den_rows = jnp.where(((den_r == h0) & (den_c == 0)) | ((den_r == h1) & (den_c == 1)), 1.0, 0.0)
                lhs = jnp.concatenate([vbd, den_rows], axis=0).astype(BF16)
                pairs[step, d, p] = (tok, lanes, q01, k01.astype(BF16), vt01, s2, bc, cm, lhs)

    row8 = lax.broadcasted_iota(jnp.int32, (SUBLANES, t), 0)
    for step in range(nsub):
        small = {}
        for d in range(N_DIR):
            a8, a_end, g8, g_max = gates[step, d]
            m_prev = m_state[d]
            m_new = jnp.maximum(a_end + m_prev, g_max)
            decay8 = jnp.exp(a_end + m_prev - m_new)
            wg8 = jnp.exp(g8 - m_new)
            m_state[d] = m_new
            small[d] = (a8, m_prev, decay8, wg8)
        weighted = {}
        for d, p in chains:
            tok, lanes, q01, k01b, vt01, s2, bc, cm, lhs = pairs[step, d, p]
            a8, m_prev = small[d][0:2]
            mx = [jnp.maximum(cm[i], m_prev[2 * p + i:2 * p + i + 1]) for i in range(2)]
            w = [jnp.exp(m_prev[2 * p + i:2 * p + i + 1] - mx[i]) for i in range(2)]
            e = [jnp.exp(-(a8[2 * p + i:2 * p + i + 1] + mx[i])) for i in range(2)]
            mr = jnp.concatenate([jnp.broadcast_to(mx[0], (t, t)), jnp.broadcast_to(mx[1], (t, t))], axis=0)
            weighted[d, p] = ((s2 * jnp.exp(bc - mr)).astype(BF16), w, e)
        for d, p in chains:
            tok, lanes, q01, k01b, vt01, s2, bc, cm, lhs = pairs[step, d, p]
            a8, m_prev, decay8, wg8 = small[d]
            sp, w, e = weighted[d, p]
            h0, h1 = 2 * p, 2 * p + 1
            st = state[d, p]
            intra = _dot(lhs, sp)
            inter = _dot_nt(st.astype(BF16), q01)
            w8 = jnp.where(row8 == h0, w[0], jnp.where(row8 == h1, w[1], 0.0))
            w_rows = jnp.concatenate([jnp.broadcast_to(w[0], (HEAD_DIM, t)), jnp.broadcast_to(w[1], (HEAD_DIM, t)),
                                      w8], axis=0)
            tot = w_rows * inter + intra
            den = jnp.concatenate([_rows(tot, PAIR_W + h0, HEAD_DIM), _rows(tot, PAIR_W + h1, HEAD_DIM)], axis=0)
            e_rows = jnp.concatenate([jnp.broadcast_to(e[0], (HEAD_DIM, t)), jnp.broadcast_to(e[1], (HEAD_DIM, t))],
                                     axis=0)
            h_t = tot[0:PAIR_W] / jnp.maximum(jnp.abs(den), e_rows)
            dirs[d][5][0, tok, lanes] = jnp.transpose(h_t)
            wg_rows = jnp.concatenate([_rows(wg8, h0, HEAD_DIM), _rows(wg8, h1, HEAD_DIM)], axis=0)
            upd = _dot(jnp.concatenate([vt01 * wg_rows, wg8], axis=0).astype(BF16), k01b)
            keep = (st_r // HEAD_DIM == st_c) | (st_r - PAIR_W == st_c + h0)
            dec = jnp.concatenate([_rows(decay8, h0, HEAD_DIM), _rows(decay8, h1, HEAD_DIM), decay8], axis=0)
            state[d, p] = dec * st + jnp.where(keep, upd, 0.0)

    for d, p in chains:
        cst_ref[0, d * M_PAIRS + p] = state[d, p]
    for d in range(N_DIR):
        mst_ref[0, d] = m_state[d]


def _mlstm(mq, mk, vt, li, lf, c0, m0, *, tb):
    b, s, _ = mq.shape
    nc = s // tb
    fwd = lambda width: pl.BlockSpec((1, tb, width), lambda bi, c: (bi, c, 0))
    bwd = lambda width: pl.BlockSpec((1, tb, width), lambda bi, c: (bi, nc - 1 - c, 0))
    fwd_t = lambda rows: pl.BlockSpec((1, rows, tb), lambda bi, c: (bi, 0, c))
    bwd_t = lambda rows: pl.BlockSpec((1, rows, tb), lambda bi, c: (bi, 0, nc - 1 - c))
    st_c = pl.BlockSpec((1, N_DIR * M_PAIRS, STATE_ROWS, PAIR_W), lambda bi, c: (bi, 0, 0, 0))
    st_m = pl.BlockSpec((1, N_DIR, SUBLANES, LANES), lambda bi, c: (bi, 0, 0, 0))
    return pl.pallas_call(
        functools.partial(_mlstm_kernel, nsub=tb // M_CHUNK),
        grid=(b, nc),
        in_specs=[fwd(M_WIDTH), fwd(M_WIDTH), fwd_t(M_WIDTH), fwd_t(GATE_ROWS), fwd_t(GATE_ROWS),
                  bwd(M_WIDTH), bwd(M_WIDTH), bwd_t(M_WIDTH), bwd_t(GATE_ROWS), bwd_t(GATE_ROWS),
                  st_c, st_m],
        out_specs=[fwd(M_WIDTH), bwd(M_WIDTH), st_c, st_m],
        out_shape=[jax.ShapeDtypeStruct((b, s, M_WIDTH), F32), jax.ShapeDtypeStruct((b, s, M_WIDTH), F32),
                   jax.ShapeDtypeStruct(c0.shape, F32), jax.ShapeDtypeStruct(m0.shape, F32)],
        compiler_params=_cparams(("parallel", "arbitrary")),
        name="mlstm_scan",
    )(mq, mk, vt, li, lf, mq, mk, vt, li, lf, c0, m0)


def _attn_kernel(*refs, seq, nq, local):
    tq = WINDOW
    if local:
        q_ref, kp_ref, kc_ref, kn_ref, vp_ref, vc_ref, vn_ref, kx_ref, vx_ref, sink_ref, o_ref = refs
        k_blocks = [kp_ref[0]] + [kc_ref[0, i * tq:(i + 1) * tq] for i in range(nq)] + [kn_ref[0]]
        v_blocks = [vp_ref[0]] + [vc_ref[0, i * tq:(i + 1) * tq] for i in range(nq)] + [vn_ref[0]]
        start = pl.program_id(1) * (nq * tq)
        row = lax.broadcasted_iota(jnp.int32, (tq, 2 * tq), 0)
        side = lax.broadcasted_iota(jnp.int32, (tq, 2 * tq), 1)
    else:
        q_ref, kx_ref, vx_ref, sink_ref, o_ref = refs
    n_keys = kx_ref.shape[1] + (3 * tq if local else 0)
    kv_of_lane = lax.broadcasted_iota(jnp.int32, (1, KV_WIDTH), 1) // HEAD_DIM
    sel = [jnp.where(kv_of_lane == g, 1.0, 0.0).astype(BF16) for g in range(A_KV_HEADS)]
    first_kv_q =lax.broadcasted_iota(jnp.int32, (A_GROUP * tq, KV_WIDTH), 1) < HEAD_DIM
    scores, sinks, values = [], [], []
    for blk in range(nq):
        if local:
            k_cat = jnp.concatenate([k_blocks[blk], k_blocks[blk + 2], k_blocks[blk + 1], kx_ref[0]], axis=0)
            v_cat = jnp.concatenate([v_blocks[blk], v_blocks[blk + 2], v_blocks[blk + 1], vx_ref[0]], axis=0)
            qpos = start + blk * tq + row
            kpos = start + blk * tq + jnp.where(side < tq, side - tq, side)
            ok = (jnp.abs(qpos - kpos) <= WINDOW) & (kpos >= 0) & (kpos < seq)
            bias = jnp.concatenate([jnp.where(ok, 0.0, NEG_INF)] * A_GROUP, axis=0)
        else:
            k_cat, v_cat = kx_ref[0], vx_ref[0]
        k_bd = jnp.concatenate([k_cat * sel[0], k_cat * sel[1]], axis=0)
        v_bd = jnp.concatenate(
            [jnp.concatenate([v_cat * sel[g], jnp.broadcast_to(sel[g], v_cat.shape)], axis=1)
             for g in range(A_KV_HEADS)], axis=0)
        qs = jnp.concatenate([q_ref[0, blk * tq:(blk + 1) * tq, j * KV_WIDTH:(j + 1) * KV_WIDTH]
                              for j in range(A_GROUP)], axis=0)
        s = _dot_nt(qs, k_bd)
        if local:
            s = jnp.concatenate([s[:, :2 * tq] + bias, s[:, 2 * tq:n_keys],
                                 s[:, n_keys:n_keys + 2 * tq] + bias, s[:, n_keys + 2 * tq:]], axis=1)
        scores.append(s)
        values.append(v_bd)
        sinks.append([jnp.concatenate(
            [jnp.broadcast_to(sink_ref[0:1, g * A_GROUP + j:g * A_GROUP + j + 1], (tq, KV_WIDTH))
             for j in range(A_GROUP)], axis=0) for g in range(A_KV_HEADS)])
    maxes = [[jnp.maximum(jnp.broadcast_to(jnp.max(s[:, g * n_keys:(g + 1) * n_keys], axis=1, keepdims=True),
                                           sk[g].shape), sk[g]) for g in range(A_KV_HEADS)]
             for s, sk in zip(scores, sinks)]
    probs = [jnp.concatenate([jnp.exp(s[:, i:i + LANES] - m[i // n_keys]).astype(BF16)
                              for i in range(0, 2 * n_keys, LANES)], axis=1) for s, m in zip(scores, maxes)]
    outs = [_dot(p, v_bd) for p, v_bd in zip(probs, values)]
    for blk, (ol, sk, m) in enumerate(zip(outs, sinks, maxes)):
        sink_term = jnp.exp(jnp.where(first_kv_q, sk[0] - m[0], sk[1] - m[1]))
        o = (ol[:, :KV_WIDTH] / (ol[:, KV_WIDTH:] + sink_term)).astype(o_ref.dtype)
        for j in range(A_GROUP):
            o_ref[0, blk * tq:(blk + 1) * tq, j * KV_WIDTH:(j + 1) * KV_WIDTH] = o[j * tq:(j + 1) * tq]


def _attention(aq, ak, av, kx, vx, sink, *, local):
    b, s, _ = aq.shape
    n_ctx = kx.shape[1]
    tq = WINDOW
    nb = s // tq
    nq = min(4, nb)
    qs = pl.BlockSpec((1, nq * tq, A_WIDTH), lambda bi, i: (bi, i, 0))
    cx = pl.BlockSpec((1, n_ctx, KV_WIDTH), lambda bi, i: (bi, 0, 0))
    sk = pl.BlockSpec((1, A_HEADS), lambda bi, i: (0, 0))
    if local:
        prv = pl.BlockSpec((1, tq, KV_WIDTH), lambda bi, i: (bi, jnp.maximum(nq * i - 1, 0), 0))
        cur = pl.BlockSpec((1, nq * tq, KV_WIDTH), lambda bi, i: (bi, i, 0))
        nxt = pl.BlockSpec((1, tq, KV_WIDTH), lambda bi, i: (bi, jnp.minimum(nq * (i + 1), nb - 1), 0))
        in_specs = [qs, prv, cur, nxt, prv, cur, nxt, cx, cx, sk]
        args = (aq, ak, ak, ak, av, av, av, kx, vx, sink)
    else:
        in_specs = [qs, cx, cx, sk]
        args = (aq, kx, vx, sink)
    return pl.pallas_call(
        functools.partial(_attn_kernel, seq=s, nq=nq, local=local),
        grid=(b, nb // nq),
        in_specs=in_specs,
        out_specs=qs,
        out_shape=jax.ShapeDtypeStruct((b, s, A_WIDTH), BF16),
        compiler_params=_cparams(("parallel", "parallel")),
        name="window_attn" if local else "ctx_attn",
    )(*args)


def _conv_prepare(up_ref, uc_ref, un_ref, ybuf, first, last):
    tc = uc_ref.shape[1]
    span = tc + 2 * CONV_HALO - SUBLANES

    def glu(u):
        return u[:, :C_WIDTH] * _sigmoid(u[:, C_WIDTH:])

    ybuf[0, 0:CONV_HALO] = jnp.where(first, 0.0, glu(up_ref[0]))
    ybuf[0, CONV_HALO:CONV_HALO + tc] = glu(uc_ref[0])
    ybuf[0, CONV_HALO + tc:2 * CONV_HALO + tc] = jnp.where(last, 0.0, glu(un_ref[0]))
    for r in range(1, SUBLANES):
        ybuf[r, 0:span] = ybuf[0, r:r + span]


def _conv_rows(r0, sub, dw_ref, db_ref, lg_ref, lb_ref, pw_ref, ybuf, c_ref):
    off = CONV_HALO - CONV_K // 2
    acc = jnp.zeros((sub // SUBLANES, SUBLANES, C_WIDTH), F32)
    for k in range(CONV_K):
        r = (off + k) % SUBLANES
        base = r0 + off + k - r
        acc = acc + ybuf[r, base:base + sub, :].reshape(sub // SUBLANES, SUBLANES, C_WIDTH) * dw_ref[k]
    y = acc.reshape(sub, C_WIDTH) + db_ref[...]
    mu = jnp.mean(y, axis=-1, keepdims=True)
    yc = y - mu
    var = jnp.mean(yc * yc, axis=-1, keepdims=True)
    z = yc * lax.rsqrt(var + EPS) * lg_ref[...] + lb_ref[...]
    z = z * _sigmoid(z)
    c_ref[r0:r0 + sub, :] = _dot(z.astype(BF16), pw_ref[...]).astype(c_ref.dtype)


def _mix_ffn_kernel(*refs, final, nt):
    (x_ref, hf_ref, hb_ref, mo_ref, a_ref, up_ref, uc_ref, un_ref, dw_ref, db_ref, lg_ref, lb_ref, pw_ref,
     hg_ref, wo_ref, g1_ref, sh2_ref, sc2_ref, g2_ref, gain2_ref, wi_ref, wf_ref) = refs[:22]
    fg_ref = refs[22] if final else None
    o_ref, ybuf, c_scr = refs[-3:]
    i = pl.program_id(0)

    @pl.when(i == 0)
    def _():
        c_scr[...] = jnp.zeros(c_scr.shape, c_scr.dtype)

    hs = hf_ref[0] + hb_ref[0]
    ri = lax.broadcasted_iota(jnp.int32, (M_WIDTH, M_WIDTH), 0) // HEAD_DIM
    ci = lax.broadcasted_iota(jnp.int32, (M_WIDTH, M_WIDTH), 1) // HEAD_DIM
    same_head = jnp.where(ri == ci, 1.0, 0.0).astype(BF16)
    ms = sum(_dot(part, same_head) for part in _split_bf16(hs * hs, 2)) * (1.0 / HEAD_DIM)
    mx = hs * lax.rsqrt(ms + EPS) * hg_ref[...] * _sigmoid(mo_ref[0])
    y = (_dot(mx.astype(BF16), wo_ref[0:M_WIDTH, :])
         + _dot(a_ref[0], wo_ref[M_WIDTH:M_WIDTH + A_WIDTH, :])
         + _dot(c_scr[...], wo_ref[M_WIDTH + A_WIDTH:, :]))
    x1 = x_ref[0] + g1_ref[0] * y
    h2 = (_rms(x1) * gain2_ref[...]) * (1.0 + sc2_ref[0]) + sh2_ref[0]
    hb2 = h2.astype(BF16)

    tile = jnp.minimum(i, pl.num_programs(0) - 2) % nt
    tm = uc_ref.shape[1]
    sub = tm // CONV_PIECES
    conv_rows = functools.partial(_conv_rows, sub=sub, dw_ref=dw_ref, db_ref=db_ref, lg_ref=lg_ref, lb_ref=lb_ref,
                                  pw_ref=pw_ref, ybuf=ybuf, c_ref=c_scr)
    _conv_prepare(up_ref, uc_ref, un_ref, ybuf, tile == 0, tile == nt - 1)
    hid = []
    for c in range(FFN_CHUNKS):
        lo, hi = c * (FFN_HIDDEN // FFN_CHUNKS), (c + 1) * (FFN_HIDDEN // FFN_CHUNKS)
        gate = _dot(hb2, wi_ref[:, lo:hi])
        up = _dot(hb2, wi_ref[:, FFN_HIDDEN + lo:FFN_HIDDEN + hi])
        hid.append((gate * _sigmoid(gate) * up).astype(BF16))
        if c < CONV_PIECES:
            conv_rows(c * sub)
    x2 = x1 + g2_ref[0] * _dot(jnp.concatenate(hid, axis=1), wf_ref[...])
    if final:
        x2 = _rms(x2) * fg_ref[...]
    o_ref[0] = x2


def _mix_ffn(x, hf, hb, mo, a, cu, conv_w, hg, wo, g1, sh2, sc2, g2, gain2, wi, wf, fg, *, tm):
    b, s, _ = x.shape
    final = fg is not None
    nt = s // tm
    n = b * nt
    nh = s // CONV_HALO
    r = tm // CONV_HALO
    cur = lambda i: jnp.maximum(i - 1, 0)
    nxt = lambda i: jnp.minimum(i, n - 1)
    tok = lambda width: pl.BlockSpec((1, tm, width), lambda i: (cur(i) // nt, cur(i) % nt, 0))
    vec = pl.BlockSpec((1, 1, D_MODEL), lambda i: (cur(i) // nt, 0, 0))
    u_cur = pl.BlockSpec((1, tm, 2 * C_WIDTH), lambda i: (nxt(i) // nt, nxt(i) % nt, 0))
    u_prv = pl.BlockSpec((1, CONV_HALO, 2 * C_WIDTH),
                         lambda i: (nxt(i) // nt, jnp.maximum((nxt(i) % nt) * r - 1, 0), 0))
    u_nxt = pl.BlockSpec((1, CONV_HALO, 2 * C_WIDTH),
                         lambda i: (nxt(i) // nt, jnp.minimum((nxt(i) % nt + 1) * r, nh - 1), 0))
    dw, db, lg, lb, pw = conv_w
    in_specs = [tok(D_MODEL), tok(M_WIDTH), tok(M_WIDTH), tok(M_WIDTH), tok(A_WIDTH), u_prv, u_cur, u_nxt,
                _const_spec((CONV_K + 1, SUBLANES, C_WIDTH)), _const_spec((1, C_WIDTH)), _const_spec((1, C_WIDTH)),
                _const_spec((1, C_WIDTH)), _const_spec((C_WIDTH, C_WIDTH)),
                _const_spec((1, M_WIDTH)), _const_spec((D_MODEL, D_MODEL)), vec, vec, vec, vec,
                _const_spec((1, D_MODEL)), _const_spec((D_MODEL, 2 * FFN_HIDDEN)),
                _const_spec((FFN_HIDDEN, D_MODEL))]
    args = [x, hf, hb, mo, a, cu, cu, cu, dw, db, lg, lb, pw, hg, wo, g1, sh2, sc2, g2, gain2, wi, wf]
    if final:
        in_specs.append(_const_spec((1, D_MODEL)))
        args.append(fg)
    return pl.pallas_call(
        functools.partial(_mix_ffn_kernel, final=final, nt=nt),
        grid=(n + 1,),
        in_specs=in_specs,
        out_specs=tok(D_MODEL),
        out_shape=jax.ShapeDtypeStruct((b, s, D_MODEL), F32),
        scratch_shapes=[pltpu.VMEM((SUBLANES, tm + 2 * CONV_HALO, C_WIDTH), F32), pltpu.VMEM((tm, C_WIDTH), BF16)],
        compiler_params=_cparams(("arbitrary",)),
        name="mix_ffn_final" if final else "mix_ffn",
    )(*args)


def _rope_tables(seq):
    half = HEAD_DIM // 4
    freqs = ROPE_THETA ** (-jnp.arange(half, dtype=F32) / half)
    t = jnp.arange(seq)
    sign = jnp.concatenate([-jnp.ones((half,), F32), jnp.ones((half,), F32)])
    cos_parts, sin_parts = [], []
    for pos in (t // GRID_W, t % GRID_W):
        ang = pos.astype(F32)[:, None] * freqs[None, :]
        c, s = jnp.cos(ang), jnp.sin(ang)
        cos_parts.append(jnp.concatenate([c, c], axis=-1))
        sin_parts.append(jnp.concatenate([s, s], axis=-1) * sign)
    cos = jnp.concatenate(cos_parts, axis=-1)
    sin = jnp.concatenate(sin_parts, axis=-1)
    rep = LANES // HEAD_DIM
    return jnp.tile(cos, (1, rep)), jnp.tile(sin, (1, rep))


def _layer_weights(w_in, gate_bias):
    o = np.cumsum((0,) + IN_SPLITS)
    col = lambda i: w_in[:, int(o[i]):int(o[i + 1])]
    scale = HEAD_DIM ** -0.5
    aq = (col(5) * scale).reshape(D_MODEL, A_HEADS, HEAD_DIM)[:, A_HEAD_ORDER].reshape(D_MODEL, A_WIDTH)
    w = jnp.concatenate([col(0), col(1) * scale, col(3), aq, col(6), col(7), col(8)], axis=1).astype(BF16)
    gates = col(4).T.reshape(N_DIR, 2, M_HEADS, D_MODEL)
    pad = ((0, 0), (0, SUBLANES - M_HEADS), (0, 0))
    li_rows = jnp.pad(gates[:, 0], pad).reshape(GATE_ROWS, D_MODEL)
    lf_rows = jnp.pad(gates[:, 1], pad).reshape(GATE_ROWS, D_MODEL)
    wt = jnp.concatenate([col(2).T, li_rows, lf_rows], axis=0).astype(BF16)
    bias = jnp.pad(gate_bias.astype(F32), ((0, 0), (0, 0), (0, SUBLANES - M_HEADS)))
    brow = jnp.concatenate([bias[:, 0].reshape(GATE_ROWS), bias[:, 1].reshape(GATE_ROWS)]).reshape(2 * GATE_ROWS, 1)
    return w, wt, brow


def kernel(x, c, ctx, c_ctx, w_mod, b_mod, norm_gain, w_in, mlstm_gate_bias, mlstm_head_gain, attn_sink,
           conv_dw_w, conv_dw_b, conv_ln_g, conv_ln_b, conv_pw_w, w_out, w_ffn_in, w_ffn_out, final_gain):
    b, seq, _ = x.shape
    n_ctx = ctx.shape[1]
    rows = -(-(b + 1) // SUBLANES) * SUBLANES
    cc = jnp.zeros((rows, D_MODEL), F32).at[:b].set(c).at[b].set(c_ctx)
    mod = _modulation(cc, w_mod, b_mod)
    cos_x, sin_x = _rope_tables(seq)
    cos_c, sin_c = cos_x[:n_ctx], sin_x[:n_ctx]
    zero_c = jnp.zeros((b, N_DIR * M_PAIRS, STATE_ROWS, PAIR_W), F32)
    zero_m = jnp.zeros((b, N_DIR, SUBLANES, LANES), F32)
    tm_x = min(512, seq)
    tm_c = min(256, n_ctx)
    tb_x = min(1024, seq)
    tb_c = min(256, n_ctx)

    for l in range(DEPTH):
        last = l == DEPTH - 1
        mx_ = mod[l, :b].reshape(b, 1, 6 * D_MODEL)
        mc_ = jnp.broadcast_to(mod[l, b].reshape(1, 1, 6 * D_MODEL), (b, 1, 6 * D_MODEL))
        part = lambda m, i: m[:, :, i * D_MODEL:(i + 1) * D_MODEL]
        w, wt, brow = _layer_weights(w_in[l], mlstm_gate_bias[l])
        gain1 = norm_gain[l, 0].reshape(1, D_MODEL)
        gain2 = norm_gain[l, 1].reshape(1, D_MODEL)
        hg = mlstm_head_gain[l].reshape(1, M_WIDTH)
        sink = attn_sink[l].reshape(1, A_HEADS).astype(F32)
        dw = jnp.broadcast_to(jnp.pad(conv_dw_w[l], ((0, 1), (0, 0)))[:, None, :], (CONV_K + 1, SUBLANES, C_WIDTH))
        db = conv_dw_b[l].reshape(1, C_WIDTH)
        lg = conv_ln_g[l].reshape(1, C_WIDTH)
        lb = conv_ln_b[l].reshape(1, C_WIDTH)
        pw = conv_pw_w[l].astype(BF16)
        wo_a = w_out[l, M_WIDTH:M_WIDTH + A_WIDTH].reshape(A_HEADS, HEAD_DIM, D_MODEL)[A_HEAD_ORDER, :, :]
        wo = jnp.concatenate([w_out[l, :M_WIDTH], wo_a.reshape(A_WIDTH, D_MODEL), w_out[l, M_WIDTH + A_WIDTH:]],
                             axis=0).astype(BF16)
        wi = w_ffn_in[l].astype(BF16)
        wf = w_ffn_out[l].astype(BF16)

        px = _inproj(x, gain1, part(mx_, 0), part(mx_, 1), w, wt, brow, cos_x, sin_x, rope=True, tm=tb_x)
        pc = _inproj(ctx, gain1, part(mc_, 0), part(mc_, 1), w, wt, brow, cos_c, sin_c, rope=False, tm=tm_c)
        mq, mk, mo, vt, li, lf, aq, ak, av, cu = px
        cmq, cmk, cmo, cvt, cli, clf, caq, cak, cav, ccu = pc

        chf, chb, cst, mst = _mlstm(cmq, cmk, cvt, cli, clf, zero_c, zero_m, tb=tb_c)
        hf, hb, _, _ = _mlstm(mq, mk, vt, li, lf, cst, mst, tb=tb_x)
        a_x = _attention(aq, ak, av, cak, cav, sink, local=True)
        conv_w = (dw, db, lg, lb, pw)
        x = _mix_ffn(x, hf, hb, mo, a_x, cu, conv_w, hg, wo, part(mx_, 2), part(mx_, 3), part(mx_, 4), part(mx_, 5),
                     gain2, wi, wf, final_gain.reshape(1, D_MODEL) if last else None, tm=tm_x)
        if not last:
            a_c = _attention(caq, cak, cav, cak, cav, sink, local=False)
            ctx = _mix_ffn(ctx, chf, chb, cmo, a_c, ccu, conv_w, hg, wo, part(mc_, 2), part(mc_, 3), part(mc_, 4),
                           part(mc_, 5), gain2, wi, wf, None, tm=tm_c)
    return x
```

```python
import functools

import numpy as np
import jax
import jax.numpy as jnp
from jax import lax
from jax.experimental import pallas as pl
from jax.experimental.pallas import tpu as pltpu

F32 = jnp.float32
BF16 = jnp.bfloat16

D_MODEL = 1024
DEPTH = 2
GRID_W = 64
HEAD_DIM = 64
M_WIDTH = D_MODEL // 4
M_HEADS = M_WIDTH // HEAD_DIM
N_DIR = 2
A_WIDTH = D_MODEL // 2
A_HEADS = A_WIDTH // HEAD_DIM
A_KV_HEADS = A_HEADS // 4
A_GROUP = A_HEADS // A_KV_HEADS
KV_WIDTH = A_KV_HEADS * HEAD_DIM
A_HEAD_ORDER = np.arange(A_HEADS).reshape(A_KV_HEADS, A_GROUP).T.reshape(-1)
WINDOW = 128
ROPE_THETA = 10000.0
C_WIDTH = D_MODEL // 4
CONV_K = 31
CONV_HALO = 16
FFN_CHUNKS = 11
CONV_PIECES = 8
FFN_HIDDEN = ((8 * D_MODEL + 3 * 256 - 1) // (3 * 256)) * 256
IN_SPLITS = (M_WIDTH, M_WIDTH, M_WIDTH, M_WIDTH, N_DIR * 2 * M_HEADS,
             A_WIDTH, KV_WIDTH, KV_WIDTH, 2 * C_WIDTH)
EPS = 1e-6
NEG_INF = -1e30

LANES = 128
SUBLANES = 8
VMEM_LIMIT = 52 * 1024 * 1024

C_MQ = 0
C_MK = C_MQ + M_WIDTH
C_MO = C_MK + M_WIDTH
C_AQ = C_MO + M_WIDTH
C_AK = C_AQ + A_WIDTH
C_AV = C_AK + KV_WIDTH
C_CU = C_AV + KV_WIDTH
N_COLS = C_CU + 2 * C_WIDTH
GATE_ROWS = N_DIR * SUBLANES
R_LI = M_WIDTH
R_LF = R_LI + GATE_ROWS
NT_ROWS = R_LF + GATE_ROWS
M_PAIRS = M_HEADS // 2
PAIR_W = 2 * HEAD_DIM
STATE_ROWS = PAIR_W + SUBLANES
M_CHUNK = LANES


def _cparams(sem):
    return pltpu.CompilerParams(dimension_semantics=sem, vmem_limit_bytes=VMEM_LIMIT)


def _const_spec(shape):
    nd = len(shape)
    return pl.BlockSpec(shape, lambda *_: (0,) * nd, pipeline_mode=pl.Buffered(1))


def _log_sigmoid(x):
    return jnp.minimum(x, 0.0) - jnp.log1p(jnp.exp(-jnp.abs(x)))


def _sigmoid(x):
    return jax.nn.sigmoid(x)


def _rms(x):
    return x * lax.rsqrt(jnp.mean(x * x, axis=-1, keepdims=True) + EPS)


def _dot(a, b):
    return jnp.dot(a, b, preferred_element_type=F32)


def _dot_nt(a, b):
    return lax.dot_general(a, b, (((1,), (1,)), ((), ())), preferred_element_type=F32)


def _split_bf16(x, parts):
    out = []
    r = x
    for _ in range(parts):
        p = r.astype(BF16)
        out.append(p)
        r = r - p.astype(F32)
    return out


def _rows(x, i, n):
    return jnp.broadcast_to(x[i:i + 1], (n, x.shape[1]))


def _mod_kernel(c_ref, w_ref, b_ref, o_ref):
    cc = c_ref[...]
    s = (cc * _sigmoid(cc)).astype(BF16)
    o_ref[0] = _dot(s, w_ref[0].astype(BF16)) + b_ref[0]


def _modulation(cc, w_mod, b_mod):
    rows = cc.shape[0]
    n = w_mod.shape[-1]
    tn = 1536
    return pl.pallas_call(
        _mod_kernel,
        grid=(DEPTH, n // tn),
        in_specs=[pl.BlockSpec((rows, D_MODEL), lambda l, j: (0, 0)),
                  pl.BlockSpec((1, D_MODEL, tn), lambda l, j: (l, 0, j)),
                  pl.BlockSpec((1, 1, tn), lambda l, j: (l, 0, j))],
        out_specs=pl.BlockSpec((1, rows, tn), lambda l, j: (l, 0, j)),
        out_shape=jax.ShapeDtypeStruct((DEPTH, rows, n), F32),
        compiler_params=_cparams(("arbitrary", "arbitrary")),
        name="modulation",
    )(cc, w_mod, b_mod.reshape(DEPTH, 1, n))


def _inproj_kernel(x_ref, gain_ref, shift_ref, scale_ref, w_ref, wt_ref, brow_ref, cos_ref, sin_ref,
                   mq_ref, mk_ref, mo_ref, vt_ref, li_ref, lf_ref, aq_ref, ak_ref, av_ref, cu_ref, *, rope):
    xf = x_ref[0]
    h = _rms(xf) * (gain_ref[...] * (1.0 + scale_ref[0])) + shift_ref[0]
    hb = h.astype(BF16)
    p = _dot(hb, w_ref[...])
    pt = _dot_nt(wt_ref[...], hb)
    mq_ref[0] = p[:, C_MQ:C_MK].astype(BF16)
    mk_ref[0] = p[:, C_MK:C_MO].astype(BF16)
    mo_ref[0] = p[:, C_MO:C_AQ]
    av_ref[0] = p[:, C_AV:C_CU].astype(BF16)
    cu_ref[0] = p[:, C_CU:N_COLS]
    vt_ref[0] = pt[0:M_WIDTH].astype(BF16)
    g = pt[R_LI:NT_ROWS] + brow_ref[...]
    li_ref[0] = g[0:GATE_ROWS]
    lf = g[GATE_ROWS:2 * GATE_ROWS]
    r = lax.broadcasted_iota(jnp.int32, lf.shape, 0)
    lf_ref[0] = jnp.where(r % SUBLANES < M_HEADS, _log_sigmoid(lf), 0.0)
    if rope:
        cos = cos_ref[...]
        sin = sin_ref[...]
        lane = lax.broadcasted_iota(jnp.int32, cos.shape, 1)
        first = (lane % 32) < 16

        def rot(z):
            sw = jnp.where(first, pltpu.roll(z, LANES - 16, 1), pltpu.roll(z, 16, 1))
            return z * cos + sw * sin

        for j in range(A_WIDTH // LANES):
            aq_ref[0, :, j * LANES:(j + 1) * LANES] = rot(
                p[:, C_AQ + j * LANES:C_AQ + (j + 1) * LANES]).astype(BF16)
        ak_ref[0] = rot(p[:, C_AK:C_AV]).astype(BF16)
    else:
        aq_ref[0] = p[:, C_AQ:C_AK].astype(BF16)
        ak_ref[0] = p[:, C_AK:C_AV].astype(BF16)


def _inproj(x, gain, shift, scale, w, wt, brow, cos, sin, *, rope, tm):
    b, s, _ = x.shape
    tok = lambda width: pl.BlockSpec((1, tm, width), lambda bi, i: (bi, i, 0))
    vec = pl.BlockSpec((1, 1, D_MODEL), lambda bi, i: (bi, 0, 0))
    tab = pl.BlockSpec((tm, LANES), lambda bi, i: (i, 0))
    tr = lambda rows: pl.BlockSpec((1, rows, tm), lambda bi, i: (bi, 0, i))
    tok_shape = lambda width, dt: jax.ShapeDtypeStruct((b, s, width), dt)
    tr_shape = lambda rows, dt: jax.ShapeDtypeStruct((b, rows, s), dt)
    out_specs = [tok(M_WIDTH), tok(M_WIDTH), tok(M_WIDTH), tr(M_WIDTH), tr(GATE_ROWS), tr(GATE_ROWS),
                 tok(A_WIDTH), tok(KV_WIDTH), tok(KV_WIDTH), tok(2 * C_WIDTH)]
    out_shape = [tok_shape(M_WIDTH, BF16), tok_shape(M_WIDTH, BF16), tok_shape(M_WIDTH, F32), tr_shape(M_WIDTH, BF16),
                 tr_shape(GATE_ROWS, F32), tr_shape(GATE_ROWS, F32),
                 tok_shape(A_WIDTH, BF16), tok_shape(KV_WIDTH, BF16), tok_shape(KV_WIDTH, BF16),
                 tok_shape(2 * C_WIDTH, F32)]
    return pl.pallas_call(
        functools.partial(_inproj_kernel, rope=rope),
        grid=(b, s // tm),
        in_specs=[tok(D_MODEL), _const_spec((1, D_MODEL)), vec, vec,
                  _const_spec((D_MODEL, N_COLS)), _const_spec((NT_ROWS, D_MODEL)),
                  _const_spec((2 * GATE_ROWS, 1)), tab, tab],
        out_specs=out_specs,
        out_shape=out_shape,
        compiler_params=_cparams(("parallel", "parallel")),
        name="inproj_rope" if rope else "inproj_ctx",
    )(x, gain, shift, scale, w, wt, brow, cos, sin)


def _mlstm_kernel(qf_ref, kf_ref, vtf_ref, lif_ref, lff_ref, qb_ref, kb_ref, vtb_ref, lib_ref, lfb_ref,
                  c0_ref, m0_ref, hf_ref, hb_ref, cst_ref, mst_ref, *, nsub):
    t = M_CHUNK

    @pl.when(pl.program_id(1) == 0)
    def _():
        cst_ref[...] = c0_ref[...]
        mst_ref[...] = m0_ref[...]

    key = lax.broadcasted_iota(jnp.int32, (t, t), 0)
    qry = lax.broadcasted_iota(jnp.int32, (t, t), 1)
    key2 = lax.broadcasted_iota(jnp.int32, (2 * t, t), 0) % t
    qry2 = lax.broadcasted_iota(jnp.int32, (2 * t, t), 1)
    first_head_lanes = lax.broadcasted_iota(jnp.int32, (t, PAIR_W), 1) < HEAD_DIM
    first_head_rows = lax.broadcasted_iota(jnp.int32, (PAIR_W, t), 0) < HEAD_DIM
    den_r = lax.broadcasted_iota(jnp.int32, (SUBLANES, 2 * t), 0)
    den_c = lax.broadcasted_iota(jnp.int32, (SUBLANES, 2 * t), 1) // t
    st_r = lax.broadcasted_iota(jnp.int32, (STATE_ROWS, PAIR_W), 0)
    st_c = lax.broadcasted_iota(jnp.int32, (STATE_ROWS, PAIR_W), 1) // HEAD_DIM
    dirs = ((qf_ref, kf_ref, vtf_ref, lif_ref, lff_ref, hf_ref),
            (qb_ref, kb_ref, vtb_ref, lib_ref, lfb_ref, hb_ref))
    chains = [(d, p) for d in range(N_DIR) for p in range(M_PAIRS)]
    visible = [key <= qry, key >= qry]
    visible2 = [key2 <= qry2, key2 >= qry2]
    ones_st = [jnp.where(v, 1.0, 0.0).astype(BF16) for v in visible]
    state = {dp: cst_ref[0, dp[0] * M_PAIRS + dp[1]] for dp in chains}
    m_state = [mst_ref[0, d] for d in range(N_DIR)]

    gates, pairs = {}, {}
    for step in range(nsub):
        for d in range(N_DIR):
            rev = d == 1
            j = nsub - 1 - step if rev else step
            tok = slice(j * t, (j + 1) * t)
            gate_rows = slice(d * SUBLANES, (d + 1) * SUBLANES)
            li8 = dirs[d][3][0, gate_rows, tok]
            lf8 = dirs[d][4][0, gate_rows, tok]
            a8 = sum(_dot(part, ones_st[d]) for part in _split_bf16(lf8, 3))
            b8 = li8 - a8
            a_end = jnp.broadcast_to(a8[:, 0:1] if rev else a8[:, t - 1:t], (SUBLANES, t))
            g8 = a_end + b8
            g_max = jnp.broadcast_to(jnp.max(g8, axis=1, keepdims=True), (SUBLANES, t))
            b_col = jnp.transpose(jnp.concatenate([b8, jnp.zeros((t - SUBLANES, t), F32)], axis=0))
            gates[step, d] = (a8, a_end, g8, g_max)
            for p in range(M_PAIRS):
                h0, h1 = 2 * p, 2 * p + 1
                lanes = slice(p * PAIR_W, (p + 1) * PAIR_W)
                q01 = dirs[d][0][0, tok, lanes].astype(BF16)
                k01 = dirs[d][1][0, tok, lanes]
                vt01 = dirs[d][2][0, lanes, tok]
                kbd = jnp.concatenate([jnp.where(first_head_lanes, k01, 0.0),
                                       jnp.where(first_head_lanes, 0.0, k01)], axis=0).astype(BF16)
                s2 = _dot_nt(kbd, q01)
                bc = jnp.concatenate([jnp.broadcast_to(b_col[:, h0:h0 + 1], (t, t)),
                                      jnp.broadcast_to(b_col[:, h1:h1 + 1], (t, t))], axis=0)
                bc = jnp.where(visible2[d], bc, -jnp.inf)
                cm = [jnp.max(bc[i * t:(i + 1) * t], axis=0, keepdims=True) for i in range(2)]
                vbd =jnp.concatenate([jnp.where(first_head_rows, vt01, 0.0),
                                       jnp.where(first_head_rows, 0.0, vt01)], axis=1)
                den_rows = jnp.where(((den_r == h0) & (den_c == 0)) | ((den_r == h1) & (den_c == 1)), 1.0, 0.0)
                lhs = jnp.concatenate([vbd, den_rows], axis=0).astype(BF16)
                pairs[step, d, p] = (tok, lanes, q01, k01.astype(BF16), vt01, s2, bc, cm, lhs)

    row8 = lax.broadcasted_iota(jnp.int32, (SUBLANES, t), 0)
    for step in range(nsub):
        small = {}
        for d in range(N_DIR):
            a8, a_end, g8, g_max = gates[step, d]
            m_prev = m_state[d]
            m_new = jnp.maximum(a_end + m_prev, g_max)
            decay8 = jnp.exp(a_end + m_prev - m_new)
            wg8 = jnp.exp(g8 - m_new)
            m_state[d] = m_new
            small[d] = (a8, m_prev, decay8, wg8)
        weighted = {}
        for d, p in chains:
            tok, lanes, q01, k01b, vt01, s2, bc, cm, lhs = pairs[step, d, p]
            a8, m_prev = small[d][0:2]
            mx = [jnp.maximum(cm[i], m_prev[2 * p + i:2 * p + i + 1]) for i in range(2)]
            w = [jnp.exp(m_prev[2 * p + i:2 * p + i + 1] - mx[i]) for i in range(2)]
            e = [jnp.exp(-(a8[2 * p + i:2 * p + i + 1] + mx[i])) for i in range(2)]
            mr = jnp.concatenate([jnp.broadcast_to(mx[0], (t, t)), jnp.broadcast_to(mx[1], (t, t))], axis=0)
            weighted[d, p] = ((s2 * jnp.exp(bc - mr)).astype(BF16), w, e)
        for d, p in chains:
            tok, lanes, q01, k01b, vt01, s2, bc, cm, lhs = pairs[step, d, p]
            a8, m_prev, decay8, wg8 = small[d]
            sp, w, e = weighted[d, p]
            h0, h1 = 2 * p, 2 * p + 1
            st = state[d, p]
            intra = _dot(lhs, sp)
            inter = _dot_nt(st.astype(BF16), q01)
            w8 = jnp.where(row8 == h0, w[0], jnp.where(row8 == h1, w[1], 0.0))
            w_rows = jnp.concatenate([jnp.broadcast_to(w[0], (HEAD_DIM, t)), jnp.broadcast_to(w[1], (HEAD_DIM, t)),
                                      w8], axis=0)
            tot = w_rows * inter + intra
            den = jnp.concatenate([_rows(tot, PAIR_W + h0, HEAD_DIM), _rows(tot, PAIR_W + h1, HEAD_DIM)], axis=0)
            e_rows = jnp.concatenate([jnp.broadcast_to(e[0], (HEAD_DIM, t)), jnp.broadcast_to(e[1], (HEAD_DIM, t))],
                                     axis=0)
            h_t = tot[0:PAIR_W] / jnp.maximum(jnp.abs(den), e_rows)
            dirs[d][5][0, tok, lanes] = jnp.transpose(h_t)
            wg_rows = jnp.concatenate([_rows(wg8, h0, HEAD_DIM), _rows(wg8, h1, HEAD_DIM)], axis=0)
            upd = _dot(jnp.concatenate([vt01 * wg_rows, wg8], axis=0).astype(BF16), k01b)
            keep = (st_r // HEAD_DIM == st_c) | (st_r - PAIR_W == st_c + h0)
            dec = jnp.concatenate([_rows(decay8, h0, HEAD_DIM), _rows(decay8, h1, HEAD_DIM), decay8], axis=0)
            state[d, p] = dec * st + jnp.where(keep, upd, 0.0)

    for d, p in chains:
        cst_ref[0, d * M_PAIRS + p] = state[d, p]
    for d in range(N_DIR):
        mst_ref[0, d] = m_state[d]


def _mlstm(mq, mk, vt, li, lf, c0, m0, *, tb):
    b, s, _ = mq.shape
    nc = s // tb
    fwd = lambda width: pl.BlockSpec((1, tb, width), lambda bi, c: (bi, c, 0))
    bwd = lambda width: pl.BlockSpec((1, tb, width), lambda bi, c: (bi, nc - 1 - c, 0))
    fwd_t = lambda rows: pl.BlockSpec((1, rows, tb), lambda bi, c: (bi, 0, c))
    bwd_t = lambda rows: pl.BlockSpec((1, rows, tb), lambda bi, c: (bi, 0, nc - 1 - c))
    st_c = pl.BlockSpec((1, N_DIR * M_PAIRS, STATE_ROWS, PAIR_W), lambda bi, c: (bi, 0, 0, 0))
    st_m = pl.BlockSpec((1, N_DIR, SUBLANES, LANES), lambda bi, c: (bi, 0, 0, 0))
    return pl.pallas_call(
        functools.partial(_mlstm_kernel, nsub=tb // M_CHUNK),
        grid=(b, nc),
        in_specs=[fwd(M_WIDTH), fwd(M_WIDTH), fwd_t(M_WIDTH), fwd_t(GATE_ROWS), fwd_t(GATE_ROWS),
                  bwd(M_WIDTH), bwd(M_WIDTH), bwd_t(M_WIDTH), bwd_t(GATE_ROWS), bwd_t(GATE_ROWS),
                  st_c, st_m],
        out_specs=[fwd(M_WIDTH), bwd(M_WIDTH), st_c, st_m],
        out_shape=[jax.ShapeDtypeStruct((b, s, M_WIDTH), F32), jax.ShapeDtypeStruct((b, s, M_WIDTH), F32),
                   jax.ShapeDtypeStruct(c0.shape, F32), jax.ShapeDtypeStruct(m0.shape, F32)],
        compiler_params=_cparams(("parallel", "arbitrary")),
        name="mlstm_scan",
    )(mq, mk, vt, li, lf, mq, mk, vt, li, lf, c0, m0)


def _attn_kernel(*refs, seq, nq, local):
    tq = WINDOW
    if local:
        q_ref, kp_ref, kc_ref, kn_ref, vp_ref, vc_ref, vn_ref, kx_ref, vx_ref, sink_ref, o_ref = refs
        k_blocks = [kp_ref[0]] + [kc_ref[0, i * tq:(i + 1) * tq] for i in range(nq)] + [kn_ref[0]]
        v_blocks = [vp_ref[0]] + [vc_ref[0, i * tq:(i + 1) * tq] for i in range(nq)] + [vn_ref[0]]
        start = pl.program_id(1) * (nq * tq)
        row = lax.broadcasted_iota(jnp.int32, (tq, 2 * tq), 0)
        side = lax.broadcasted_iota(jnp.int32, (tq, 2 * tq), 1)
    else:
        q_ref, kx_ref, vx_ref, sink_ref, o_ref = refs
    n_keys = kx_ref.shape[1] + (3 * tq if local else 0)
    kv_of_lane = lax.broadcasted_iota(jnp.int32, (1, KV_WIDTH), 1) // HEAD_DIM
    sel = [jnp.where(kv_of_lane == g, 1.0, 0.0).astype(BF16) for g in range(A_KV_HEADS)]
    first_kv_q =lax.broadcasted_iota(jnp.int32, (A_GROUP * tq, KV_WIDTH), 1) < HEAD_DIM
    scores, sinks, values = [], [], []
    for blk in range(nq):
        if local:
            k_cat = jnp.concatenate([k_blocks[blk], k_blocks[blk + 2], k_blocks[blk + 1], kx_ref[0]], axis=0)
            v_cat = jnp.concatenate([v_blocks[blk], v_blocks[blk + 2], v_blocks[blk + 1], vx_ref[0]], axis=0)
            qpos = start + blk * tq + row
            kpos = start + blk * tq + jnp.where(side < tq, side - tq, side)
            ok = (jnp.abs(qpos - kpos) <= WINDOW) & (kpos >= 0) & (kpos < seq)
            bias = jnp.concatenate([jnp.where(ok, 0.0, NEG_INF)] * A_GROUP, axis=0)
        else:
            k_cat, v_cat = kx_ref[0], vx_ref[0]
        k_bd = jnp.concatenate([k_cat * sel[0], k_cat * sel[1]], axis=0)
        v_bd = jnp.concatenate(
            [jnp.concatenate([v_cat * sel[g], jnp.broadcast_to(sel[g], v_cat.shape)], axis=1)
             for g in range(A_KV_HEADS)], axis=0)
        qs = jnp.concatenate([q_ref[0, blk * tq:(blk + 1) * tq, j * KV_WIDTH:(j + 1) * KV_WIDTH]
                              for j in range(A_GROUP)], axis=0)
        s = _dot_nt(qs, k_bd)
        if local:
            s = jnp.concatenate([s[:, :2 * tq] + bias, s[:, 2 * tq:n_keys],
                                 s[:, n_keys:n_keys + 2 * tq] + bias, s[:, n_keys + 2 * tq:]], axis=1)
        scores.append(s)
        values.append(v_bd)
        sinks.append([jnp.concatenate(
            [jnp.broadcast_to(sink_ref[0:1, g * A_GROUP + j:g * A_GROUP + j + 1], (tq, KV_WIDTH))
             for j in range(A_GROUP)], axis=0) for g in range(A_KV_HEADS)])
    maxes = [[jnp.maximum(jnp.broadcast_to(jnp.max(s[:, g * n_keys:(g + 1) * n_keys], axis=1, keepdims=True),
                                           sk[g].shape), sk[g]) for g in range(A_KV_HEADS)]
             for s, sk in zip(scores, sinks)]
    probs = [jnp.concatenate([jnp.exp(s[:, i:i + LANES] - m[i // n_keys]).astype(BF16)
                              for i in range(0, 2 * n_keys, LANES)], axis=1) for s, m in zip(scores, maxes)]
    outs = [_dot(p, v_bd) for p, v_bd in zip(probs, values)]
    for blk, (ol, sk, m) in enumerate(zip(outs, sinks, maxes)):
        sink_term = jnp.exp(jnp.where(first_kv_q, sk[0] - m[0], sk[1] - m[1]))
        o = (ol[:, :KV_WIDTH] / (ol[:, KV_WIDTH:] + sink_term)).astype(o_ref.dtype)
        for j in range(A_GROUP):
            o_ref[0, blk * tq:(blk + 1) * tq, j * KV_WIDTH:(j + 1) * KV_WIDTH] = o[j * tq:(j + 1) * tq]


def _attention(aq, ak, av, kx, vx, sink, *, local):
    b, s, _ = aq.shape
    n_ctx = kx.shape[1]
    tq = WINDOW
    nb = s // tq
    nq = min(8, nb)
    qs = pl.BlockSpec((1, nq * tq, A_WIDTH), lambda bi, i: (bi, i, 0))
    cx = pl.BlockSpec((1, n_ctx, KV_WIDTH), lambda bi, i: (bi, 0, 0))
    sk = pl.BlockSpec((1, A_HEADS), lambda bi, i: (0, 0))
    if local:
        prv = pl.BlockSpec((1, tq, KV_WIDTH), lambda bi, i: (bi, jnp.maximum(nq * i - 1, 0), 0))
        cur = pl.BlockSpec((1, nq * tq, KV_WIDTH), lambda bi, i: (bi, i, 0))
        nxt = pl.BlockSpec((1, tq, KV_WIDTH), lambda bi, i: (bi, jnp.minimum(nq * (i + 1), nb - 1), 0))
        in_specs = [qs, prv, cur, nxt, prv, cur, nxt, cx, cx, sk]
        args = (aq, ak, ak, ak, av, av, av, kx, vx, sink)
    else:
        in_specs = [qs, cx, cx, sk]
        args = (aq, kx, vx, sink)
    return pl.pallas_call(
        functools.partial(_attn_kernel, seq=s, nq=nq, local=local),
        grid=(b, nb // nq),
        in_specs=in_specs,
        out_specs=qs,
        out_shape=jax.ShapeDtypeStruct((b, s, A_WIDTH), BF16),
        compiler_params=_cparams(("parallel", "parallel")),
        name="window_attn" if local else "ctx_attn",
    )(*args)


def _conv_prepare(up_ref, uc_ref, un_ref, ybuf, first, last):
    tc = uc_ref.shape[1]
    span = tc + 2 * CONV_HALO - SUBLANES

    def glu(u):
        return u[:, :C_WIDTH] * _sigmoid(u[:, C_WIDTH:])

    ybuf[0, 0:CONV_HALO] = jnp.where(first, 0.0, glu(up_ref[0]))
    ybuf[0, CONV_HALO:CONV_HALO + tc] = glu(uc_ref[0])
    ybuf[0, CONV_HALO + tc:2 * CONV_HALO + tc] = jnp.where(last, 0.0, glu(un_ref[0]))
    for r in range(1, SUBLANES):
        ybuf[r, 0:span] = ybuf[0, r:r + span]


def _conv_rows(r0, sub, dw_ref, db_ref, lg_ref, lb_ref, pw_ref, ybuf, c_ref):
    off = CONV_HALO - CONV_K // 2
    acc = jnp.zeros((sub // SUBLANES, SUBLANES, C_WIDTH), F32)
    for k in range(CONV_K):
        r = (off + k) % SUBLANES
        base = r0 + off + k - r
        acc = acc + ybuf[r, base:base + sub, :].reshape(sub // SUBLANES, SUBLANES, C_WIDTH) * dw_ref[k]
    y = acc.reshape(sub, C_WIDTH) + db_ref[...]
    mu = jnp.mean(y, axis=-1, keepdims=True)
    yc = y - mu
    var = jnp.mean(yc * yc, axis=-1, keepdims=True)
    z = yc * lax.rsqrt(var + EPS) * lg_ref[...] + lb_ref[...]
    z = z * _sigmoid(z)
    c_ref[r0:r0 + sub, :] = _dot(z.astype(BF16), pw_ref[...]).astype(c_ref.dtype)


def _mix_ffn_kernel(*refs, final, nt):
    (x_ref, hf_ref, hb_ref, mo_ref, a_ref, up_ref, uc_ref, un_ref, dw_ref, db_ref, lg_ref, lb_ref, pw_ref,
     hg_ref, wo_ref, g1_ref, sh2_ref, sc2_ref, g2_ref, gain2_ref, wi_ref, wf_ref) = refs[:22]
    fg_ref = refs[22] if final else None
    o_ref, ybuf, c_scr = refs[-3:]
    i = pl.program_id(0)

    @pl.when(i == 0)
    def _():
        c_scr[...] = jnp.zeros(c_scr.shape, c_scr.dtype)

    hs = hf_ref[0] + hb_ref[0]
    ri = lax.broadcasted_iota(jnp.int32, (M_WIDTH, M_WIDTH), 0) // HEAD_DIM
    ci = lax.broadcasted_iota(jnp.int32, (M_WIDTH, M_WIDTH), 1) // HEAD_DIM
    same_head = jnp.where(ri == ci, 1.0, 0.0).astype(BF16)
    ms = sum(_dot(part, same_head) for part in _split_bf16(hs * hs, 2)) * (1.0 / HEAD_DIM)
    mx = hs * lax.rsqrt(ms + EPS) * hg_ref[...] * _sigmoid(mo_ref[0])
    y = (_dot(mx.astype(BF16), wo_ref[0:M_WIDTH, :])
         + _dot(a_ref[0], wo_ref[M_WIDTH:M_WIDTH + A_WIDTH, :])
         + _dot(c_scr[...], wo_ref[M_WIDTH + A_WIDTH:, :]))
    x1 = x_ref[0] + g1_ref[0] * y
    h2 = _rms(x1) * (gain2_ref[...] * (1.0 + sc2_ref[0])) + sh2_ref[0]
    hb2 = h2.astype(BF16)

    tile = jnp.minimum(i, pl.num_programs(0) - 2) % nt
    tm = uc_ref.shape[1]
    sub = tm // CONV_PIECES
    conv_rows = functools.partial(_conv_rows, sub=sub, dw_ref=dw_ref, db_ref=db_ref, lg_ref=lg_ref, lb_ref=lb_ref,
                                  pw_ref=pw_ref, ybuf=ybuf, c_ref=c_scr)
    _conv_prepare(up_ref, uc_ref, un_ref, ybuf, tile == 0, tile == nt - 1)
    hid = []
    for c in range(FFN_CHUNKS):
        lo, hi = c * (FFN_HIDDEN // FFN_CHUNKS), (c + 1) * (FFN_HIDDEN // FFN_CHUNKS)
        gate = _dot(hb2, wi_ref[:, lo:hi])
        up = _dot(hb2, wi_ref[:, FFN_HIDDEN + lo:FFN_HIDDEN + hi])
        hid.append((gate * _sigmoid(gate) * up).astype(BF16))
        if c < CONV_PIECES:
            conv_rows(c * sub)
    x2 = x1 + g2_ref[0] * _dot(jnp.concatenate(hid, axis=1), wf_ref[...])
    if final:
        x2 = _rms(x2) * fg_ref[...]
    o_ref[0] = x2


def _mix_ffn(x, hf, hb, mo, a, cu, conv_w, hg, wo, g1, sh2, sc2, g2, gain2, wi, wf, fg, *, tm):
    b, s, _ = x.shape
    final = fg is not None
    nt = s // tm
    n = b * nt
    nh = s // CONV_HALO
    r = tm // CONV_HALO
    cur = lambda i: jnp.maximum(i - 1, 0)
    nxt = lambda i: jnp.minimum(i, n - 1)
    tok = lambda width: pl.BlockSpec((1, tm, width), lambda i: (cur(i) // nt, cur(i) % nt, 0))
    vec = pl.BlockSpec((1, 1, D_MODEL), lambda i: (cur(i) // nt, 0, 0))
    u_cur = pl.BlockSpec((1, tm, 2 * C_WIDTH), lambda i: (nxt(i) // nt, nxt(i) % nt, 0))
    u_prv = pl.BlockSpec((1, CONV_HALO, 2 * C_WIDTH),
                         lambda i: (nxt(i) // nt, jnp.maximum((nxt(i) % nt) * r - 1, 0), 0))
    u_nxt = pl.BlockSpec((1, CONV_HALO, 2 * C_WIDTH),
                         lambda i: (nxt(i) // nt, jnp.minimum((nxt(i) % nt + 1) * r, nh - 1), 0))
    dw, db, lg, lb, pw = conv_w
    in_specs = [tok(D_MODEL), tok(M_WIDTH), tok(M_WIDTH), tok(M_WIDTH), tok(A_WIDTH), u_prv, u_cur, u_nxt,
                _const_spec((CONV_K + 1, SUBLANES, C_WIDTH)), _const_spec((1, C_WIDTH)), _const_spec((1, C_WIDTH)),
                _const_spec((1, C_WIDTH)), _const_spec((C_WIDTH, C_WIDTH)),
                _const_spec((1, M_WIDTH)), _const_spec((D_MODEL, D_MODEL)), vec, vec, vec, vec,
                _const_spec((1, D_MODEL)), _const_spec((D_MODEL, 2 * FFN_HIDDEN)),
                _const_spec((FFN_HIDDEN, D_MODEL))]
    args = [x, hf, hb, mo, a, cu, cu, cu, dw, db, lg, lb, pw, hg, wo, g1, sh2, sc2, g2, gain2, wi, wf]
    if final:
        in_specs.append(_const_spec((1, D_MODEL)))
        args.append(fg)
    return pl.pallas_call(
        functools.partial(_mix_ffn_kernel, final=final, nt=nt),
        grid=(n + 1,),
        in_specs=in_specs,
        out_specs=tok(D_MODEL),
        out_shape=jax.ShapeDtypeStruct((b, s, D_MODEL), F32),
        scratch_shapes=[pltpu.VMEM((SUBLANES, tm + 2 * CONV_HALO, C_WIDTH), F32), pltpu.VMEM((tm, C_WIDTH), BF16)],
        compiler_params=_cparams(("arbitrary",)),
        name="mix_ffn_final" if final else "mix_ffn",
    )(*args)


def _rope_tables(seq):
    half = HEAD_DIM // 4
    freqs = ROPE_THETA ** (-jnp.arange(half, dtype=F32) / half)
    t = jnp.arange(seq)
    sign = jnp.concatenate([-jnp.ones((half,), F32), jnp.ones((half,), F32)])
    cos_parts, sin_parts = [], []
    for pos in (t // GRID_W, t % GRID_W):
        ang = pos.astype(F32)[:, None] * freqs[None, :]
        c, s = jnp.cos(ang), jnp.sin(ang)
        cos_parts.append(jnp.concatenate([c, c], axis=-1))
        sin_parts.append(jnp.concatenate([s, s], axis=-1) * sign)
    cos = jnp.concatenate(cos_parts, axis=-1)
    sin = jnp.concatenate(sin_parts, axis=-1)
    rep = LANES // HEAD_DIM
    return jnp.tile(cos, (1, rep)), jnp.tile(sin, (1, rep))


def _layer_weights(w_in, gate_bias):
    o = np.cumsum((0,) + IN_SPLITS)
    col = lambda i: w_in[:, int(o[i]):int(o[i + 1])]
    scale = HEAD_DIM ** -0.5
    aq = (col(5) * scale).reshape(D_MODEL, A_HEADS, HEAD_DIM)[:, A_HEAD_ORDER].reshape(D_MODEL, A_WIDTH)
    w = jnp.concatenate([col(0), col(1) * scale, col(3), aq, col(6), col(7), col(8)], axis=1).astype(BF16)
    gates = col(4).T.reshape(N_DIR, 2, M_HEADS, D_MODEL)
    pad = ((0, 0), (0, SUBLANES - M_HEADS), (0, 0))
    li_rows = jnp.pad(gates[:, 0], pad).reshape(GATE_ROWS, D_MODEL)
    lf_rows = jnp.pad(gates[:, 1], pad).reshape(GATE_ROWS, D_MODEL)
    wt = jnp.concatenate([col(2).T, li_rows, lf_rows], axis=0).astype(BF16)
    bias = jnp.pad(gate_bias.astype(F32), ((0, 0), (0, 0), (0, SUBLANES - M_HEADS)))
    brow = jnp.concatenate([bias[:, 0].reshape(GATE_ROWS), bias[:, 1].reshape(GATE_ROWS)]).reshape(2 * GATE_ROWS, 1)
    return w, wt, brow


def kernel(x, c, ctx, c_ctx, w_mod, b_mod, norm_gain, w_in, mlstm_gate_bias, mlstm_head_gain, attn_sink,
           conv_dw_w, conv_dw_b, conv_ln_g, conv_ln_b, conv_pw_w, w_out, w_ffn_in, w_ffn_out, final_gain):
    b, seq, _ = x.shape
    n_ctx = ctx.shape[1]
    rows = -(-(b + 1) // SUBLANES) * SUBLANES
    cc = jnp.zeros((rows, D_MODEL), F32).at[:b].set(c).at[b].set(c_ctx)
    mod = _modulation(cc, w_mod, b_mod)
    cos_x, sin_x = _rope_tables(seq)
    cos_c, sin_c = cos_x[:n_ctx], sin_x[:n_ctx]
    zero_c = jnp.zeros((b, N_DIR * M_PAIRS, STATE_ROWS, PAIR_W), F32)
    zero_m = jnp.zeros((b, N_DIR, SUBLANES, LANES), F32)
    tm_in = min(1024, seq)
    tb_x = min(2048, seq)
    tm_x = min(512, seq)
    tm_c = tb_c = min(256, n_ctx)

    for l in range(DEPTH):
        last = l == DEPTH - 1
        mx_ = mod[l, :b].reshape(b, 1, 6 * D_MODEL)
        mc_ = jnp.broadcast_to(mod[l, b].reshape(1, 1, 6 * D_MODEL), (b, 1, 6 * D_MODEL))
        part = lambda m, i: m[:, :, i * D_MODEL:(i + 1) * D_MODEL]
        w, wt, brow = _layer_weights(w_in[l], mlstm_gate_bias[l])
        gain1 = norm_gain[l, 0].reshape(1, D_MODEL)
        gain2 = norm_gain[l, 1].reshape(1, D_MODEL)
        hg = mlstm_head_gain[l].reshape(1, M_WIDTH)
        sink = attn_sink[l].reshape(1, A_HEADS).astype(F32)
        dw = jnp.broadcast_to(jnp.pad(conv_dw_w[l], ((0, 1), (0, 0)))[:, None, :], (CONV_K + 1, SUBLANES, C_WIDTH))
        db = conv_dw_b[l].reshape(1, C_WIDTH)
        lg = conv_ln_g[l].reshape(1, C_WIDTH)
        lb = conv_ln_b[l].reshape(1, C_WIDTH)
        pw = conv_pw_w[l].astype(BF16)
        wo_a = w_out[l, M_WIDTH:M_WIDTH + A_WIDTH].reshape(A_HEADS, HEAD_DIM, D_MODEL)[A_HEAD_ORDER, :, :]
        wo = jnp.concatenate([w_out[l, :M_WIDTH], wo_a.reshape(A_WIDTH, D_MODEL), w_out[l, M_WIDTH + A_WIDTH:]],
                             axis=0).astype(BF16)
        wi = w_ffn_in[l].astype(BF16)
        wf = w_ffn_out[l].astype(BF16)

        px = _inproj(x, gain1, part(mx_, 0), part(mx_, 1), w, wt, brow, cos_x, sin_x, rope=True, tm=tm_in)
        pc = _inproj(ctx, gain1, part(mc_, 0), part(mc_, 1), w, wt, brow, cos_c, sin_c, rope=False, tm=tm_c)
        mq, mk, mo, vt, li, lf, aq, ak, av, cu = px
        cmq, cmk, cmo, cvt, cli, clf, caq, cak, cav, ccu = pc

        chf, chb, cst, mst = _mlstm(cmq, cmk, cvt, cli, clf, zero_c, zero_m, tb=tb_c)
        hf, hb, _, _ = _mlstm(mq, mk, vt, li, lf, cst, mst, tb=tb_x)
        a_x = _attention(aq, ak, av, cak, cav, sink, local=True)
        conv_w = (dw, db, lg, lb, pw)
        x = _mix_ffn(x, hf, hb, mo, a_x, cu, conv_w, hg, wo, part(mx_, 2), part(mx_, 3), part(mx_, 4), part(mx_, 5),
                     gain2, wi, wf, final_gain.reshape(1, D_MODEL) if last else None, tm=tm_x)
        if not last:
            a_c = _attention(caq, cak, cav, cak, cav, sink, local=False)
            ctx = _mix_ffn(ctx, chf, chb, cmo, a_c, ccu, conv_w, hg, wo, part(mc_, 2), part(mc_, 3), part(mc_, 4),
                           part(mc_, 5), gain2, wi, wf, None, tm=tm_c)
    return x
```

```python
import functools

import numpy as np
import jax
import jax.numpy as jnp
from jax import lax
from jax.experimental import pallas as pl
from jax.experimental.pallas import tpu as pltpu

F32 = jnp.float32
BF16 = jnp.bfloat16

D_MODEL = 1024
DEPTH = 2
GRID_W = 64
HEAD_DIM = 64
M_WIDTH = D_MODEL // 4
M_HEADS = M_WIDTH // HEAD_DIM
N_DIR = 2
A_WIDTH = D_MODEL // 2
A_HEADS = A_WIDTH // HEAD_DIM
A_KV_HEADS = A_HEADS // 4
A_GROUP = A_HEADS // A_KV_HEADS
KV_WIDTH = A_KV_HEADS * HEAD_DIM
A_HEAD_ORDER = np.arange(A_HEADS).reshape(A_KV_HEADS, A_GROUP).T.reshape(-1)
WINDOW = 128
ROPE_THETA = 10000.0
C_WIDTH = D_MODEL // 4
CONV_K = 31
CONV_HALO = 16
FFN_CHUNKS = 11
CONV_PIECES = 8
FFN_HIDDEN = ((8 * D_MODEL + 3 * 256 - 1) // (3 * 256)) * 256
IN_SPLITS = (M_WIDTH, M_WIDTH, M_WIDTH, M_WIDTH, N_DIR * 2 * M_HEADS,
             A_WIDTH, KV_WIDTH, KV_WIDTH, 2 * C_WIDTH)
EPS = 1e-6
NEG_INF = -1e30
LOG2E = 1.4426950408889634

LANES = 128
SUBLANES = 8
VMEM_LIMIT = 52 * 1024 * 1024

C_MQ = 0
C_MK = C_MQ + M_WIDTH
C_MO = C_MK + M_WIDTH
C_AQ = C_MO + M_WIDTH
C_AK = C_AQ + A_WIDTH
C_AV = C_AK + KV_WIDTH
C_CU = C_AV + KV_WIDTH
N_COLS = C_CU + 2 * C_WIDTH
GATE_ROWS = N_DIR * SUBLANES
R_LI = M_WIDTH
R_LF = R_LI + GATE_ROWS
NT_ROWS = R_LF + GATE_ROWS
M_PAIRS = M_HEADS // 2
PAIR_W = 2 * HEAD_DIM
STATE_ROWS = PAIR_W + SUBLANES
M_CHUNK = LANES


def _cparams(sem):
    return pltpu.CompilerParams(dimension_semantics=sem, vmem_limit_bytes=VMEM_LIMIT)


def _const_spec(shape):
    nd = len(shape)
    return pl.BlockSpec(shape, lambda *_: (0,) * nd, pipeline_mode=pl.Buffered(1))


def _log_sigmoid(x):
    return jnp.minimum(x, 0.0) - jnp.log1p(jnp.exp(-jnp.abs(x)))


def _sigmoid(x):
    return jax.nn.sigmoid(x)


def _rms(x):
    return x * lax.rsqrt(jnp.mean(x * x, axis=-1, keepdims=True) + EPS)


def _dot(a, b):
    return jnp.dot(a, b, preferred_element_type=F32)


def _dot_nt(a, b):
    return lax.dot_general(a, b, (((1,), (1,)), ((), ())), preferred_element_type=F32)


def _split_bf16(x, parts):
    out = []
    r = x
    for _ in range(parts):
        p = r.astype(BF16)
        out.append(p)
        r = r - p.astype(F32)
    return out


def _rows(x, i, n):
    return jnp.broadcast_to(x[i:i + 1], (n, x.shape[1]))


def _mod_kernel(c_ref, w_ref, b_ref, o_ref):
    cc = c_ref[...]
    s = (cc * _sigmoid(cc)).astype(BF16)
    o_ref[0] = _dot(s, w_ref[0].astype(BF16)) + b_ref[0]


def _modulation(cc, w_mod, b_mod):
    rows = cc.shape[0]
    n = w_mod.shape[-1]
    tn = 1536
    return pl.pallas_call(
        _mod_kernel,
        grid=(DEPTH, n // tn),
        in_specs=[pl.BlockSpec((rows, D_MODEL), lambda l, j: (0, 0)),
                  pl.BlockSpec((1, D_MODEL, tn), lambda l, j: (l, 0, j)),
                  pl.BlockSpec((1, 1, tn), lambda l, j: (l, 0, j))],
        out_specs=pl.BlockSpec((1, rows, tn), lambda l, j: (l, 0, j)),
        out_shape=jax.ShapeDtypeStruct((DEPTH, rows, n), F32),
        compiler_params=_cparams(("arbitrary", "arbitrary")),
        name="modulation",
    )(cc, w_mod, b_mod.reshape(DEPTH, 1, n))


def _inproj_kernel(x_ref, gain_ref, shift_ref, scale_ref, w_ref, wt_ref, brow_ref, cos_ref, sin_ref,
                   mq_ref, mk_ref, mo_ref, vt_ref, li_ref, lf_ref, aq_ref, ak_ref, av_ref, cu_ref, *, rope):
    xf = x_ref[0]
    h = _rms(xf) * (gain_ref[...] * (1.0 + scale_ref[0])) + shift_ref[0]
    hb = h.astype(BF16)
    p = _dot(hb, w_ref[...])
    pt = _dot_nt(wt_ref[...], hb)
    mq_ref[0] = p[:, C_MQ:C_MK].astype(BF16)
    mk_ref[0] = p[:, C_MK:C_MO].astype(BF16)
    mo_ref[0] = p[:, C_MO:C_AQ]
    av_ref[0] = p[:, C_AV:C_CU].astype(BF16)
    cu_ref[0] = p[:, C_CU:N_COLS]
    vt_ref[0] = pt[0:M_WIDTH].astype(BF16)
    g = pt[R_LI:NT_ROWS] + brow_ref[...]
    li_ref[0] = g[0:GATE_ROWS]
    lf = g[GATE_ROWS:2 * GATE_ROWS]
    r = lax.broadcasted_iota(jnp.int32, lf.shape, 0)
    lf_ref[0] = jnp.where(r % SUBLANES < M_HEADS, _log_sigmoid(lf), 0.0)
    if rope:
        cos = cos_ref[...]
        sin = sin_ref[...]
        lane = lax.broadcasted_iota(jnp.int32, cos.shape, 1)
        first = (lane % 32) < 16

        def rot(z):
            sw = jnp.where(first, pltpu.roll(z, LANES - 16, 1), pltpu.roll(z, 16, 1))
            return z * cos + sw * sin

        for j in range(A_WIDTH // LANES):
            aq_ref[0, :, j * LANES:(j + 1) * LANES] = rot(
                p[:, C_AQ + j * LANES:C_AQ + (j + 1) * LANES]).astype(BF16)
        ak_ref[0] = rot(p[:, C_AK:C_AV]).astype(BF16)
    else:
        aq_ref[0] = p[:, C_AQ:C_AK].astype(BF16)
        ak_ref[0] = p[:, C_AK:C_AV].astype(BF16)


def _inproj(x, gain, shift, scale, w, wt, brow, cos, sin, *, rope, tm):
    b, s, _ = x.shape
    tok = lambda width: pl.BlockSpec((1, tm, width), lambda bi, i: (bi, i, 0))
    vec = pl.BlockSpec((1, 1, D_MODEL), lambda bi, i: (bi, 0, 0))
    tab = pl.BlockSpec((tm, LANES), lambda bi, i: (i, 0))
    tr = lambda rows: pl.BlockSpec((1, rows, tm), lambda bi, i: (bi, 0, i))
    tok_shape = lambda width, dt: jax.ShapeDtypeStruct((b, s, width), dt)
    tr_shape = lambda rows, dt: jax.ShapeDtypeStruct((b, rows, s), dt)
    out_specs = [tok(M_WIDTH), tok(M_WIDTH), tok(M_WIDTH), tr(M_WIDTH), tr(GATE_ROWS), tr(GATE_ROWS),
                 tok(A_WIDTH), tok(KV_WIDTH), tok(KV_WIDTH), tok(2 * C_WIDTH)]
    out_shape = [tok_shape(M_WIDTH, BF16), tok_shape(M_WIDTH, BF16), tok_shape(M_WIDTH, F32), tr_shape(M_WIDTH, BF16),
                 tr_shape(GATE_ROWS, F32), tr_shape(GATE_ROWS, F32),
                 tok_shape(A_WIDTH, BF16), tok_shape(KV_WIDTH, BF16), tok_shape(KV_WIDTH, BF16),
                 tok_shape(2 * C_WIDTH, F32)]
    return pl.pallas_call(
        functools.partial(_inproj_kernel, rope=rope),
        grid=(b, s // tm),
        in_specs=[tok(D_MODEL), _const_spec((1, D_MODEL)), vec, vec,
                  _const_spec((D_MODEL, N_COLS)), _const_spec((NT_ROWS, D_MODEL)),
                  _const_spec((2 * GATE_ROWS, 1)), tab, tab],
        out_specs=out_specs,
        out_shape=out_shape,
        compiler_params=_cparams(("parallel", "parallel")),
        name="inproj_rope" if rope else "inproj_ctx",
    )(x, gain, shift, scale, w, wt, brow, cos, sin)


def _mlstm_kernel(qf_ref, kf_ref, vtf_ref, lif_ref, lff_ref, qb_ref, kb_ref, vtb_ref, lib_ref, lfb_ref,
                  c0_ref, m0_ref, hf_ref, hb_ref, cst_ref, mst_ref, *, nsub):
    t = M_CHUNK

    @pl.when(pl.program_id(1) == 0)
    def _():
        cst_ref[...] = c0_ref[...]
        mst_ref[...] = m0_ref[...]

    key = lax.broadcasted_iota(jnp.int32, (t, t), 0)
    qry = lax.broadcasted_iota(jnp.int32, (t, t), 1)
    key2 = lax.broadcasted_iota(jnp.int32, (2 * t, t), 0) % t
    qry2 = lax.broadcasted_iota(jnp.int32, (2 * t, t), 1)
    first_head_lanes = lax.broadcasted_iota(jnp.int32, (t, PAIR_W), 1) < HEAD_DIM
    first_head_rows = lax.broadcasted_iota(jnp.int32, (PAIR_W, t), 0) < HEAD_DIM
    den_r = lax.broadcasted_iota(jnp.int32, (SUBLANES, 2 * t), 0)
    den_c = lax.broadcasted_iota(jnp.int32, (SUBLANES, 2 * t), 1) // t
    st_r = lax.broadcasted_iota(jnp.int32, (STATE_ROWS, PAIR_W), 0)
    st_c = lax.broadcasted_iota(jnp.int32, (STATE_ROWS, PAIR_W), 1) // HEAD_DIM
    dirs = ((qf_ref, kf_ref, vtf_ref, lif_ref, lff_ref, hf_ref),
            (qb_ref, kb_ref, vtb_ref, lib_ref, lfb_ref, hb_ref))
    chains = [(d, p) for d in range(N_DIR) for p in range(M_PAIRS)]
    visible = [key <= qry, key >= qry]
    visible2 = [key2 <= qry2, key2 >= qry2]
    ones_st = [jnp.where(v, 1.0, 0.0).astype(BF16) for v in visible]
    state = {dp: cst_ref[0, dp[0] * M_PAIRS + dp[1]] for dp in chains}
    m_state = [mst_ref[0, d] for d in range(N_DIR)]

    gates, pairs = {}, {}
    for step in range(nsub):
        for d in range(N_DIR):
            rev = d == 1
            j = nsub - 1 - step if rev else step
            tok = slice(j * t, (j + 1) * t)
            gate_rows = slice(d * SUBLANES, (d + 1) * SUBLANES)
            li8 = dirs[d][3][0, gate_rows, tok]
            lf8 = dirs[d][4][0, gate_rows, tok]
            a8 = sum(_dot(part, ones_st[d]) for part in _split_bf16(lf8, 3))
            b8 = li8 - a8
            a_end = jnp.broadcast_to(a8[:, 0:1] if rev else a8[:, t - 1:t], (SUBLANES, t))
            g8 = a_end + b8
            g_max = jnp.broadcast_to(jnp.max(g8, axis=1, keepdims=True), (SUBLANES, t))
            b_col = jnp.transpose(jnp.concatenate([b8, jnp.zeros((t - SUBLANES, t), F32)], axis=0))
            gates[step, d] = (a8, a_end, g8, g_max)
            for p in range(M_PAIRS):
                h0, h1 = 2 * p, 2 * p + 1
                lanes = slice(p * PAIR_W, (p + 1) * PAIR_W)
                q01 = dirs[d][0][0, tok, lanes].astype(BF16)
                k01 = dirs[d][1][0, tok, lanes]
                vt01 = dirs[d][2][0, lanes, tok]
                kbd = jnp.concatenate([jnp.where(first_head_lanes, k01, 0.0),
                                       jnp.where(first_head_lanes, 0.0, k01)], axis=0).astype(BF16)
                s2 = _dot_nt(kbd, q01)
                bc = jnp.concatenate([jnp.broadcast_to(b_col[:, h0:h0 + 1], (t, t)),
                                      jnp.broadcast_to(b_col[:, h1:h1 + 1], (t, t))], axis=0)
                bc = jnp.where(visible2[d], bc, -jnp.inf)
                cm = [jnp.max(bc[i * t:(i + 1) * t], axis=0, keepdims=True) for i in range(2)]
                vbd =jnp.concatenate([jnp.where(first_head_rows, vt01, 0.0),
                                       jnp.where(first_head_rows, 0.0, vt01)], axis=1)
                den_rows = jnp.where(((den_r == h0) & (den_c == 0)) | ((den_r == h1) & (den_c == 1)), 1.0, 0.0)
                lhs = jnp.concatenate([vbd, den_rows], axis=0).astype(BF16)
                pairs[step, d, p] = (tok, lanes, q01, k01.astype(BF16), vt01, s2, bc, cm, lhs)

    row8 = lax.broadcasted_iota(jnp.int32, (SUBLANES, t), 0)
    for step in range(nsub):
        small = {}
        for d in range(N_DIR):
            a8, a_end, g8, g_max = gates[step, d]
            m_prev = m_state[d]
            m_new = jnp.maximum(a_end + m_prev, g_max)
            decay8 = jnp.exp(a_end + m_prev - m_new)
            wg8 = jnp.exp(g8 - m_new)
            m_state[d] = m_new
            small[d] = (a8, m_prev, decay8, wg8)
        weighted = {}
        for d, p in chains:
            tok, lanes, q01, k01b, vt01, s2, bc, cm, lhs = pairs[step, d, p]
            a8, m_prev = small[d][0:2]
            mx = [jnp.maximum(cm[i], m_prev[2 * p + i:2 * p + i + 1]) for i in range(2)]
            w = [jnp.exp(m_prev[2 * p + i:2 * p + i + 1] - mx[i]) for i in range(2)]
            e = [jnp.exp(-(a8[2 * p + i:2 * p + i + 1] + mx[i])) for i in range(2)]
            mr = jnp.concatenate([jnp.broadcast_to(mx[0], (t, t)), jnp.broadcast_to(mx[1], (t, t))], axis=0)
            weighted[d, p] = ((s2 * jnp.exp(bc - mr)).astype(BF16), w, e)
        for d, p in chains:
            tok, lanes, q01, k01b, vt01, s2, bc, cm, lhs = pairs[step, d, p]
            a8, m_prev, decay8, wg8 = small[d]
            sp, w, e = weighted[d, p]
            h0, h1 = 2 * p, 2 * p + 1
            st = state[d, p]
            intra = _dot(lhs, sp)
            inter = _dot_nt(st.astype(BF16), q01)
            w8 = jnp.where(row8 == h0, w[0], jnp.where(row8 == h1, w[1], 0.0))
            w_rows = jnp.concatenate([jnp.broadcast_to(w[0], (HEAD_DIM, t)), jnp.broadcast_to(w[1], (HEAD_DIM, t)),
                                      w8], axis=0)
            tot = w_rows * inter + intra
            den = jnp.concatenate([_rows(tot, PAIR_W + h0, HEAD_DIM), _rows(tot, PAIR_W + h1, HEAD_DIM)], axis=0)
            e_rows = jnp.concatenate([jnp.broadcast_to(e[0], (HEAD_DIM, t)), jnp.broadcast_to(e[1], (HEAD_DIM, t))],
                                     axis=0)
            h_t = tot[0:PAIR_W] / jnp.maximum(jnp.abs(den), e_rows)
            dirs[d][5][0, tok, lanes] = jnp.transpose(h_t)
            wg_rows = jnp.concatenate([_rows(wg8, h0, HEAD_DIM), _rows(wg8, h1, HEAD_DIM)], axis=0)
            upd = _dot(jnp.concatenate([vt01 * wg_rows, wg8], axis=0).astype(BF16), k01b)
            keep = (st_r // HEAD_DIM == st_c) | (st_r - PAIR_W == st_c + h0)
            dec = jnp.concatenate([_rows(decay8, h0, HEAD_DIM), _rows(decay8, h1, HEAD_DIM), decay8], axis=0)
            state[d, p] = dec * st + jnp.where(keep, upd, 0.0)

    for d, p in chains:
        cst_ref[0, d * M_PAIRS + p] = state[d, p]
    for d in range(N_DIR):
        mst_ref[0, d] = m_state[d]


def _mlstm(mq, mk, vt, li, lf, c0, m0, *, tb):
    b, s, _ = mq.shape
    nc = s // tb
    fwd = lambda width: pl.BlockSpec((1, tb, width), lambda bi, c: (bi, c, 0))
    bwd = lambda width: pl.BlockSpec((1, tb, width), lambda bi, c: (bi, nc - 1 - c, 0))
    fwd_t = lambda rows: pl.BlockSpec((1, rows, tb), lambda bi, c: (bi, 0, c))
    bwd_t = lambda rows: pl.BlockSpec((1, rows, tb), lambda bi, c: (bi, 0, nc - 1 - c))
    st_c = pl.BlockSpec((1, N_DIR * M_PAIRS, STATE_ROWS, PAIR_W), lambda bi, c: (bi, 0, 0, 0))
    st_m = pl.BlockSpec((1, N_DIR, SUBLANES, LANES), lambda bi, c: (bi, 0, 0, 0))
    return pl.pallas_call(
        functools.partial(_mlstm_kernel, nsub=tb // M_CHUNK),
        grid=(b, nc),
        in_specs=[fwd(M_WIDTH), fwd(M_WIDTH), fwd_t(M_WIDTH), fwd_t(GATE_ROWS), fwd_t(GATE_ROWS),
                  bwd(M_WIDTH), bwd(M_WIDTH), bwd_t(M_WIDTH), bwd_t(GATE_ROWS), bwd_t(GATE_ROWS),
                  st_c, st_m],
        out_specs=[fwd(M_WIDTH), bwd(M_WIDTH), st_c, st_m],
        out_shape=[jax.ShapeDtypeStruct((b, s, M_WIDTH), F32), jax.ShapeDtypeStruct((b, s, M_WIDTH), F32),
                   jax.ShapeDtypeStruct(c0.shape, F32), jax.ShapeDtypeStruct(m0.shape, F32)],
        compiler_params=_cparams(("parallel", "arbitrary")),
        name="mlstm_scan",
    )(mq, mk, vt, li, lf, mq, mk, vt, li, lf, c0, m0)


def _attn_kernel(*refs, seq, nq, local):
    tq = WINDOW
    if local:
        q_ref, kp_ref, kc_ref, kn_ref, vp_ref, vc_ref, vn_ref, kx_ref, vx_ref, sink_ref, o_ref = refs
        k_blocks = [kp_ref[0]] + [kc_ref[0, i * tq:(i + 1) * tq] for i in range(nq)] + [kn_ref[0]]
        v_blocks = [vp_ref[0]] + [vc_ref[0, i * tq:(i + 1) * tq] for i in range(nq)] + [vn_ref[0]]
        start = pl.program_id(1) * (nq * tq)
        row = lax.broadcasted_iota(jnp.int32, (tq, 2 * tq), 0)
        side = lax.broadcasted_iota(jnp.int32, (tq, 2 * tq), 1)
    else:
        q_ref, kx_ref, vx_ref, sink_ref, o_ref = refs
    n_keys = kx_ref.shape[1] + (3 * tq if local else 0)
    kv_of_lane = lax.broadcasted_iota(jnp.int32, (1, KV_WIDTH), 1) // HEAD_DIM
    sel = [jnp.where(kv_of_lane == g, 1.0, 0.0).astype(BF16) for g in range(A_KV_HEADS)]
    first_kv_q =lax.broadcasted_iota(jnp.int32, (A_GROUP * tq, KV_WIDTH), 1) < HEAD_DIM
    scores, sinks, values = [], [], []
    for blk in range(nq):
        if local:
            k_cat = jnp.concatenate([k_blocks[blk], k_blocks[blk + 2], k_blocks[blk + 1], kx_ref[0]], axis=0)
            v_cat = jnp.concatenate([v_blocks[blk], v_blocks[blk + 2], v_blocks[blk + 1], vx_ref[0]], axis=0)
            qpos = start + blk * tq + row
            kpos = start + blk * tq + jnp.where(side < tq, side - tq, side)
            ok = (jnp.abs(qpos - kpos) <= WINDOW) & (kpos >= 0) & (kpos < seq)
            bias = jnp.concatenate([jnp.where(ok, 0.0, NEG_INF)] * A_GROUP, axis=0)
        else:
            k_cat, v_cat = kx_ref[0], vx_ref[0]
        k_bd = jnp.concatenate([k_cat * sel[0], k_cat * sel[1]], axis=0)
        v_bd = jnp.concatenate(
            [jnp.concatenate([v_cat * sel[g], jnp.broadcast_to(sel[g], v_cat.shape)], axis=1)
             for g in range(A_KV_HEADS)], axis=0)
        qs = jnp.concatenate([q_ref[0, blk * tq:(blk + 1) * tq, j * KV_WIDTH:(j + 1) * KV_WIDTH]
                              for j in range(A_GROUP)], axis=0)
        s = _dot_nt(qs, k_bd)
        if local:
            s = jnp.concatenate([s[:, :2 * tq] + bias, s[:, 2 * tq:n_keys],
                                 s[:, n_keys:n_keys + 2 * tq] + bias, s[:, n_keys + 2 * tq:]], axis=1)
        scores.append(s)
        values.append(v_bd)
        sinks.append([jnp.concatenate(
            [jnp.broadcast_to(sink_ref[0:1, g * A_GROUP + j:g * A_GROUP + j + 1], (tq, KV_WIDTH))
             for j in range(A_GROUP)], axis=0) for g in range(A_KV_HEADS)])
    maxes = [[jnp.maximum(jnp.broadcast_to(jnp.max(s[:, g * n_keys:(g + 1) * n_keys], axis=1, keepdims=True),
                                           sk[g].shape), sk[g]) for g in range(A_KV_HEADS)]
             for s, sk in zip(scores, sinks)]
    probs = [jnp.concatenate([jnp.exp2(s[:, i:i + LANES] - m[i // n_keys]).astype(BF16)
                              for i in range(0, 2 * n_keys, LANES)], axis=1) for s, m in zip(scores, maxes)]
    outs = [_dot(p, v_bd) for p, v_bd in zip(probs, values)]
    for blk, (ol, sk, m) in enumerate(zip(outs, sinks, maxes)):
        sink_term = jnp.exp2(jnp.where(first_kv_q, sk[0] - m[0], sk[1] - m[1]))
        o = (ol[:, :KV_WIDTH] / (ol[:, KV_WIDTH:] + sink_term)).astype(o_ref.dtype)
        for j in range(A_GROUP):
            o_ref[0, blk * tq:(blk + 1) * tq, j * KV_WIDTH:(j + 1) * KV_WIDTH] = o[j * tq:(j + 1) * tq]


def _attention(aq, ak, av, kx, vx, sink, *, local):
    b, s, _ = aq.shape
    n_ctx = kx.shape[1]
    tq = WINDOW
    nb = s // tq
    nq = min(8, nb)
    qs = pl.BlockSpec((1, nq * tq, A_WIDTH), lambda bi, i: (bi, i, 0))
    cx = pl.BlockSpec((1, n_ctx, KV_WIDTH), lambda bi, i: (bi, 0, 0))
    sk = pl.BlockSpec((1, A_HEADS), lambda bi, i: (0, 0))
    if local:
        prv = pl.BlockSpec((1, tq, KV_WIDTH), lambda bi, i: (bi, jnp.maximum(nq * i - 1, 0), 0))
        cur = pl.BlockSpec((1, nq * tq, KV_WIDTH), lambda bi, i: (bi, i, 0))
        nxt = pl.BlockSpec((1, tq, KV_WIDTH), lambda bi, i: (bi, jnp.minimum(nq * (i + 1), nb - 1), 0))
        in_specs = [qs, prv, cur, nxt, prv, cur, nxt, cx, cx, sk]
        args = (aq, ak, ak, ak, av, av, av, kx, vx, sink)
    else:
        in_specs = [qs, cx, cx, sk]
        args = (aq, kx, vx, sink)
    return pl.pallas_call(
        functools.partial(_attn_kernel, seq=s, nq=nq, local=local),
        grid=(b, nb // nq),
        in_specs=in_specs,
        out_specs=qs,
        out_shape=jax.ShapeDtypeStruct((b, s, A_WIDTH), BF16),
        compiler_params=_cparams(("parallel", "parallel")),
        name="window_attn" if local else "ctx_attn",
    )(*args)


def _conv_prepare(up_ref, uc_ref, un_ref, ybuf, first, last):
    tc = uc_ref.shape[1]
    span = tc + 2 * CONV_HALO - SUBLANES

    def glu(u):
        return u[:, :C_WIDTH] * _sigmoid(u[:, C_WIDTH:])

    ybuf[0, 0:CONV_HALO] = jnp.where(first, 0.0, glu(up_ref[0]))
    ybuf[0, CONV_HALO:CONV_HALO + tc] = glu(uc_ref[0])
    ybuf[0, CONV_HALO + tc:2 * CONV_HALO + tc] = jnp.where(last, 0.0, glu(un_ref[0]))
    for r in range(1, SUBLANES):
        ybuf[r, 0:span] = ybuf[0, r:r + span]


def _conv_rows(r0, sub, dw_ref, db_ref, lg_ref, lb_ref, pw_ref, ybuf, c_ref):
    off = CONV_HALO - CONV_K // 2
    acc = jnp.zeros((sub // SUBLANES, SUBLANES, C_WIDTH), F32)
    for k in range(CONV_K):
        r = (off + k) % SUBLANES
        base = r0 + off + k - r
        acc = acc + ybuf[r, base:base + sub, :].reshape(sub // SUBLANES, SUBLANES, C_WIDTH) * dw_ref[k]
    y = acc.reshape(sub, C_WIDTH) + db_ref[...]
    mu = jnp.mean(y, axis=-1, keepdims=True)
    yc = y - mu
    var = jnp.mean(yc * yc, axis=-1, keepdims=True)
    z = yc * lax.rsqrt(var + EPS) * lg_ref[...] + lb_ref[...]
    z = z * _sigmoid(z)
    c_ref[r0:r0 + sub, :] = _dot(z.astype(BF16), pw_ref[...]).astype(c_ref.dtype)


def _mix_ffn_kernel(*refs, final, nt):
    (x_ref, hf_ref, hb_ref, mo_ref, a_ref, up_ref, uc_ref, un_ref, dw_ref, db_ref, lg_ref, lb_ref, pw_ref,
     hg_ref, wo_ref, g1_ref, sh2_ref, sc2_ref, g2_ref, gain2_ref, wi_ref, wf_ref) = refs[:22]
    fg_ref = refs[22] if final else None
    o_ref, ybuf, c_scr = refs[-3:]
    i = pl.program_id(0)

    @pl.when(i == 0)
    def _():
        c_scr[...] = jnp.zeros(c_scr.shape, c_scr.dtype)

    hs = hf_ref[0] + hb_ref[0]
    ri = lax.broadcasted_iota(jnp.int32, (M_WIDTH, M_WIDTH), 0) // HEAD_DIM
    ci = lax.broadcasted_iota(jnp.int32, (M_WIDTH, M_WIDTH), 1) // HEAD_DIM
    same_head = jnp.where(ri == ci, 1.0, 0.0).astype(BF16)
    ms = sum(_dot(part, same_head) for part in _split_bf16(hs * hs, 2)) * (1.0 / HEAD_DIM)
    mx = hs * lax.rsqrt(ms + EPS) * hg_ref[...] * _sigmoid(mo_ref[0])
    y = (_dot(mx.astype(BF16), wo_ref[0:M_WIDTH, :])
         + _dot(a_ref[0], wo_ref[M_WIDTH:M_WIDTH + A_WIDTH, :])
         + _dot(c_scr[...], wo_ref[M_WIDTH + A_WIDTH:, :]))
    x1 = x_ref[0] + g1_ref[0] * y
    h2 = _rms(x1) * (gain2_ref[...] * (1.0 + sc2_ref[0])) + sh2_ref[0]
    hb2 = h2.astype(BF16)

    tile = jnp.minimum(i, pl.num_programs(0) - 2) % nt
    tm = uc_ref.shape[1]
    sub = tm // CONV_PIECES
    conv_rows = functools.partial(_conv_rows, sub=sub, dw_ref=dw_ref, db_ref=db_ref, lg_ref=lg_ref, lb_ref=lb_ref,
                                  pw_ref=pw_ref, ybuf=ybuf, c_ref=c_scr)
    _conv_prepare(up_ref, uc_ref, un_ref, ybuf, tile == 0, tile == nt - 1)
    hid = []
    for c in range(FFN_CHUNKS):
        lo, hi = c * (FFN_HIDDEN // FFN_CHUNKS), (c + 1) * (FFN_HIDDEN // FFN_CHUNKS)
        gate = _dot(hb2, wi_ref[:, lo:hi])
        up = _dot(hb2, wi_ref[:, FFN_HIDDEN + lo:FFN_HIDDEN + hi])
        hid.append((gate * _sigmoid(gate) * up).astype(BF16))
        if c < CONV_PIECES:
            conv_rows(c * sub)
    x2 = x1 + g2_ref[0] * _dot(jnp.concatenate(hid, axis=1), wf_ref[...])
    if final:
        x2 = _rms(x2) * fg_ref[...]
    o_ref[0] = x2


def _mix_ffn(x, hf, hb, mo, a, cu, conv_w, hg, wo, g1, sh2, sc2, g2, gain2, wi, wf, fg, *, tm):
    b, s, _ = x.shape
    final = fg is not None
    nt = s // tm
    n = b * nt
    nh = s // CONV_HALO
    r = tm // CONV_HALO
    cur = lambda i: jnp.maximum(i - 1, 0)
    nxt = lambda i: jnp.minimum(i, n - 1)
    tok = lambda width: pl.BlockSpec((1, tm, width), lambda i: (cur(i) // nt, cur(i) % nt, 0))
    vec = pl.BlockSpec((1, 1, D_MODEL), lambda i: (cur(i) // nt, 0, 0))
    u_cur = pl.BlockSpec((1, tm, 2 * C_WIDTH), lambda i: (nxt(i) // nt, nxt(i) % nt, 0))
    u_prv = pl.BlockSpec((1, CONV_HALO, 2 * C_WIDTH),
                         lambda i: (nxt(i) // nt, jnp.maximum((nxt(i) % nt) * r - 1, 0), 0))
    u_nxt = pl.BlockSpec((1, CONV_HALO, 2 * C_WIDTH),
                         lambda i: (nxt(i) // nt, jnp.minimum((nxt(i) % nt + 1) * r, nh - 1), 0))
    dw, db, lg, lb, pw = conv_w
    in_specs = [tok(D_MODEL), tok(M_WIDTH), tok(M_WIDTH), tok(M_WIDTH), tok(A_WIDTH), u_prv, u_cur, u_nxt,
                _const_spec((CONV_K + 1, SUBLANES, C_WIDTH)), _const_spec((1, C_WIDTH)), _const_spec((1, C_WIDTH)),
                _const_spec((1, C_WIDTH)), _const_spec((C_WIDTH, C_WIDTH)),
                _const_spec((1, M_WIDTH)), _const_spec((D_MODEL, D_MODEL)), vec, vec, vec, vec,
                _const_spec((1, D_MODEL)), _const_spec((D_MODEL, 2 * FFN_HIDDEN)),
                _const_spec((FFN_HIDDEN, D_MODEL))]
    args = [x, hf, hb, mo, a, cu, cu, cu, dw, db, lg, lb, pw, hg, wo, g1, sh2, sc2, g2, gain2, wi, wf]
    if final:
        in_specs.append(_const_spec((1, D_MODEL)))
        args.append(fg)
    return pl.pallas_call(
        functools.partial(_mix_ffn_kernel, final=final, nt=nt),
        grid=(n + 1,),
        in_specs=in_specs,
        out_specs=tok(D_MODEL),
        out_shape=jax.ShapeDtypeStruct((b, s, D_MODEL), F32),
        scratch_shapes=[pltpu.VMEM((SUBLANES, tm + 2 * CONV_HALO, C_WIDTH), F32), pltpu.VMEM((tm, C_WIDTH), BF16)],
        compiler_params=_cparams(("arbitrary",)),
        name="mix_ffn_final" if final else "mix_ffn",
    )(*args)


def _rope_tables(seq):
    half = HEAD_DIM // 4
    freqs = ROPE_THETA ** (-jnp.arange(half, dtype=F32) / half)
    t = jnp.arange(seq)
    sign = jnp.concatenate([-jnp.ones((half,), F32), jnp.ones((half,), F32)])
    cos_parts, sin_parts = [], []
    for pos in (t // GRID_W, t % GRID_W):
        ang = pos.astype(F32)[:, None] * freqs[None, :]
        c, s = jnp.cos(ang), jnp.sin(ang)
        cos_parts.append(jnp.concatenate([c, c], axis=-1))
        sin_parts.append(jnp.concatenate([s, s], axis=-1) * sign)
    cos = jnp.concatenate(cos_parts, axis=-1)
    sin = jnp.concatenate(sin_parts, axis=-1)
    rep = LANES // HEAD_DIM
    return jnp.tile(cos, (1, rep)), jnp.tile(sin, (1, rep))


def _layer_weights(w_in, gate_bias):
    o = np.cumsum((0,) + IN_SPLITS)
    col = lambda i: w_in[:, int(o[i]):int(o[i + 1])]
    scale = HEAD_DIM ** -0.5
    aq = (col(5) * (scale * LOG2E)).reshape(D_MODEL, A_HEADS, HEAD_DIM)[:, A_HEAD_ORDER].reshape(D_MODEL, A_WIDTH)
    w = jnp.concatenate([col(0), col(1) * scale, col(3), aq, col(6), col(7), col(8)], axis=1).astype(BF16)
    gates = col(4).T.reshape(N_DIR, 2, M_HEADS, D_MODEL)
    pad = ((0, 0), (0, SUBLANES - M_HEADS), (0, 0))
    li_rows = jnp.pad(gates[:, 0], pad).reshape(GATE_ROWS, D_MODEL)
    lf_rows = jnp.pad(gates[:, 1], pad).reshape(GATE_ROWS, D_MODEL)
    wt = jnp.concatenate([col(2).T, li_rows, lf_rows], axis=0).astype(BF16)
    bias = jnp.pad(gate_bias.astype(F32), ((0, 0), (0, 0), (0, SUBLANES - M_HEADS)))
    brow = jnp.concatenate([bias[:, 0].reshape(GATE_ROWS), bias[:, 1].reshape(GATE_ROWS)]).reshape(2 * GATE_ROWS, 1)
    return w, wt, brow


def kernel(x, c, ctx, c_ctx, w_mod, b_mod, norm_gain, w_in, mlstm_gate_bias, mlstm_head_gain, attn_sink,
           conv_dw_w, conv_dw_b, conv_ln_g, conv_ln_b, conv_pw_w, w_out, w_ffn_in, w_ffn_out, final_gain):
    b, seq, _ = x.shape
    n_ctx = ctx.shape[1]
    rows = -(-(b + 1) // SUBLANES) * SUBLANES
    cc = jnp.zeros((rows, D_MODEL), F32).at[:b].set(c).at[b].set(c_ctx)
    mod = _modulation(cc, w_mod, b_mod)
    cos_x, sin_x = _rope_tables(seq)
    cos_c, sin_c = cos_x[:n_ctx], sin_x[:n_ctx]
    zero_c = jnp.zeros((b, N_DIR * M_PAIRS, STATE_ROWS, PAIR_W), F32)
    zero_m = jnp.zeros((b, N_DIR, SUBLANES, LANES), F32)
    tm_in = min(1024, seq)
    tb_x = min(2048, seq)
    tm_x = min(512, seq)
    tm_c = tb_c = min(256, n_ctx)

    for l in range(DEPTH):
        last = l == DEPTH - 1
        mx_ = mod[l, :b].reshape(b, 1, 6 * D_MODEL)
        mc_ = jnp.broadcast_to(mod[l, b].reshape(1, 1, 6 * D_MODEL), (b, 1, 6 * D_MODEL))
        part = lambda m, i: m[:, :, i * D_MODEL:(i + 1) * D_MODEL]
        w, wt, brow = _layer_weights(w_in[l], mlstm_gate_bias[l])
        gain1 = norm_gain[l, 0].reshape(1, D_MODEL)
        gain2 = norm_gain[l, 1].reshape(1, D_MODEL)
        hg = mlstm_head_gain[l].reshape(1, M_WIDTH)
        sink = attn_sink[l].reshape(1, A_HEADS).astype(F32) * LOG2E
        dw = jnp.broadcast_to(jnp.pad(conv_dw_w[l], ((0, 1), (0, 0)))[:, None, :], (CONV_K + 1, SUBLANES, C_WIDTH))
        db = conv_dw_b[l].reshape(1, C_WIDTH)
        lg = conv_ln_g[l].reshape(1, C_WIDTH)
        lb = conv_ln_b[l].reshape(1, C_WIDTH)
        pw = conv_pw_w[l].astype(BF16)
        wo_a = w_out[l, M_WIDTH:M_WIDTH + A_WIDTH].reshape(A_HEADS, HEAD_DIM, D_MODEL)[A_HEAD_ORDER, :, :]
        wo = jnp.concatenate([w_out[l, :M_WIDTH], wo_a.reshape(A_WIDTH, D_MODEL), w_out[l, M_WIDTH + A_WIDTH:]],
                             axis=0).astype(BF16)
        wi = w_ffn_in[l].astype(BF16)
        wf = w_ffn_out[l].astype(BF16)

        px = _inproj(x, gain1, part(mx_, 0), part(mx_, 1), w, wt, brow, cos_x, sin_x, rope=True, tm=tm_in)
        pc = _inproj(ctx, gain1, part(mc_, 0), part(mc_, 1), w, wt, brow, cos_c, sin_c, rope=False, tm=tm_c)
        mq, mk, mo, vt, li, lf, aq, ak, av, cu = px
        cmq, cmk, cmo, cvt, cli, clf, caq, cak, cav, ccu = pc

        chf, chb, cst, mst = _mlstm(cmq, cmk, cvt, cli, clf, zero_c, zero_m, tb=tb_c)
        hf, hb, _, _ = _mlstm(mq, mk, vt, li, lf, cst, mst, tb=tb_x)
        a_x = _attention(aq, ak, av, cak, cav, sink, local=True)
        conv_w = (dw, db, lg, lb, pw)
        x = _mix_ffn(x, hf, hb, mo, a_x, cu, conv_w, hg, wo, part(mx_, 2), part(mx_, 3), part(mx_, 4), part(mx_, 5),
                     gain2, wi, wf, final_gain.reshape(1, D_MODEL) if last else None, tm=tm_x)
        if not last:
            a_c = _attention(caq, cak, cav, cak, cav, sink, local=False)
            ctx = _mix_ffn(ctx, chf, chb, cmo, a_c, ccu, conv_w, hg, wo, part(mc_, 2), part(mc_, 3), part(mc_, 4),
                           part(mc_, 5), gain2, wi, wf, None, tm=tm_c)
    return x
```

```python
import functools

import numpy as np
import jax
import jax.numpy as jnp
from jax import lax
from jax.experimental import pallas as pl
from jax.experimental.pallas import tpu as pltpu

F32 = jnp.float32
BF16 = jnp.bfloat16

D_MODEL = 1024
DEPTH = 2
GRID_W = 64
HEAD_DIM = 64
M_WIDTH = D_MODEL // 4
M_HEADS = M_WIDTH // HEAD_DIM
N_DIR = 2
A_WIDTH = D_MODEL // 2
A_HEADS = A_WIDTH // HEAD_DIM
A_KV_HEADS = A_HEADS // 4
A_GROUP = A_HEADS // A_KV_HEADS
KV_WIDTH = A_KV_HEADS * HEAD_DIM
A_HEAD_ORDER = np.arange(A_HEADS).reshape(A_KV_HEADS, A_GROUP).T.reshape(-1)
WINDOW = 128
ROPE_THETA = 10000.0
C_WIDTH = D_MODEL // 4
CONV_K = 31
CONV_HALO = 16
FFN_CHUNKS = 11
CONV_PIECES = 8
FFN_HIDDEN = ((8 * D_MODEL + 3 * 256 - 1) // (3 * 256)) * 256
IN_SPLITS = (M_WIDTH, M_WIDTH, M_WIDTH, M_WIDTH, N_DIR * 2 * M_HEADS,
             A_WIDTH, KV_WIDTH, KV_WIDTH, 2 * C_WIDTH)
EPS = 1e-6
NEG_INF = -1e30
LOG2E = 1.4426950408889634

LANES = 128
SUBLANES = 8
VMEM_LIMIT = 52 * 1024 * 1024

C_MQ = 0
C_MK = C_MQ + M_WIDTH
C_MO = C_MK + M_WIDTH
C_AQ = C_MO + M_WIDTH
C_AK = C_AQ + A_WIDTH
C_AV = C_AK + KV_WIDTH
C_CU = C_AV + KV_WIDTH
N_COLS = C_CU + 2 * C_WIDTH
GATE_ROWS = N_DIR * SUBLANES
R_LI = M_WIDTH
R_LF = R_LI + GATE_ROWS
NT_ROWS = R_LF + GATE_ROWS
M_PAIRS = M_HEADS // 2
PAIR_W = 2 * HEAD_DIM
STATE_ROWS = PAIR_W + SUBLANES
M_CHUNK = LANES


def _cparams(sem):
    return pltpu.CompilerParams(dimension_semantics=sem, vmem_limit_bytes=VMEM_LIMIT)


def _const_spec(shape):
    nd = len(shape)
    return pl.BlockSpec(shape, lambda *_: (0,) * nd, pipeline_mode=pl.Buffered(1))


def _log_sigmoid(x):
    return jnp.minimum(x, 0.0) - jnp.log1p(jnp.exp(-jnp.abs(x)))


def _sigmoid(x):
    return jax.nn.sigmoid(x)


def _rms(x):
    return x * lax.rsqrt(jnp.mean(x * x, axis=-1, keepdims=True) + EPS)


def _dot(a, b):
    return jnp.dot(a, b, preferred_element_type=F32)


def _dot_nt(a, b):
    return lax.dot_general(a, b, (((1,), (1,)), ((), ())), preferred_element_type=F32)


def _split_bf16(x, parts):
    out = []
    r = x
    for _ in range(parts):
        p = r.astype(BF16)
        out.append(p)
        r = r - p.astype(F32)
    return out


def _rows(x, i, n):
    return jnp.broadcast_to(x[i:i + 1], (n, x.shape[1]))


def _mod_kernel(c_ref, w_ref, b_ref, o_ref):
    cc = c_ref[...]
    s = (cc * _sigmoid(cc)).astype(BF16)
    o_ref[0] = _dot(s, w_ref[0].astype(BF16)) + b_ref[0]


def _modulation(cc, w_mod, b_mod):
    rows = cc.shape[0]
    n = w_mod.shape[-1]
    tn = 1536
    return pl.pallas_call(
        _mod_kernel,
        grid=(DEPTH, n // tn),
        in_specs=[pl.BlockSpec((rows, D_MODEL), lambda l, j: (0, 0)),
                  pl.BlockSpec((1, D_MODEL, tn), lambda l, j: (l, 0, j)),
                  pl.BlockSpec((1, 1, tn), lambda l, j: (l, 0, j))],
        out_specs=pl.BlockSpec((1, rows, tn), lambda l, j: (l, 0, j)),
        out_shape=jax.ShapeDtypeStruct((DEPTH, rows, n), F32),
        compiler_params=_cparams(("arbitrary", "arbitrary")),
        name="modulation",
    )(cc, w_mod, b_mod.reshape(DEPTH, 1, n))


def _cast_kernel(w_ref, o_ref):
    o_ref[...] = w_ref[...].astype(o_ref.dtype)


def _to_bf16(w, row_block):
    depth, rows, cols = w.shape
    spec = pl.BlockSpec((1, row_block, cols), lambda l, i: (l, i, 0))
    return pl.pallas_call(
        _cast_kernel,
        grid=(depth, rows // row_block),
        in_specs=[spec],
        out_specs=spec,
        out_shape=jax.ShapeDtypeStruct(w.shape, BF16),
        compiler_params=_cparams(("parallel", "parallel")),
        name="weight_cast",
    )(w)


def _layer_spec(shape, layer):
    nd = len(shape)
    return pl.BlockSpec((None,) + shape, lambda *_: (layer,) + (0,) * nd, pipeline_mode=pl.Buffered(1))


def _inproj_kernel(x_ref, gain_ref, shift_ref, scale_ref, w_ref, wt_ref, brow_ref, cos_ref, sin_ref,
                   mq_ref, mk_ref, mo_ref, vt_ref, li_ref, lf_ref, aq_ref, ak_ref, av_ref, cu_ref, *, rope):
    xf = x_ref[0]
    h = _rms(xf) * (gain_ref[...] * (1.0 + scale_ref[0])) + shift_ref[0]
    hb = h.astype(BF16)
    p = _dot(hb, w_ref[...])
    pt = _dot_nt(wt_ref[...], hb)
    mq_ref[0] = p[:, C_MQ:C_MK].astype(BF16)
    mk_ref[0] = p[:, C_MK:C_MO].astype(BF16)
    mo_ref[0] = p[:, C_MO:C_AQ]
    av_ref[0] = p[:, C_AV:C_CU].astype(BF16)
    cu_ref[0] = p[:, C_CU:N_COLS]
    vt_ref[0] = pt[0:M_WIDTH].astype(BF16)
    g = pt[R_LI:NT_ROWS] + brow_ref[...]
    li_ref[0] = g[0:GATE_ROWS]
    lf = g[GATE_ROWS:2 * GATE_ROWS]
    r = lax.broadcasted_iota(jnp.int32, lf.shape, 0)
    lf_ref[0] = jnp.where(r % SUBLANES < M_HEADS, _log_sigmoid(lf), 0.0)
    if rope:
        cos = cos_ref[...]
        sin = sin_ref[...]
        lane = lax.broadcasted_iota(jnp.int32, cos.shape, 1)
        first = (lane % 32) < 16

        def rot(z):
            sw = jnp.where(first, pltpu.roll(z, LANES - 16, 1), pltpu.roll(z, 16, 1))
            return z * cos + sw * sin

        for j in range(A_WIDTH // LANES):
            aq_ref[0, :, j * LANES:(j + 1) * LANES] = rot(
                p[:, C_AQ + j * LANES:C_AQ + (j + 1) * LANES]).astype(BF16)
        ak_ref[0] = rot(p[:, C_AK:C_AV]).astype(BF16)
    else:
        aq_ref[0] = p[:, C_AQ:C_AK].astype(BF16)
        ak_ref[0] = p[:, C_AK:C_AV].astype(BF16)


def _inproj(x, gain, shift, scale, w, wt, brow, cos, sin, *, rope, tm):
    b, s, _ = x.shape
    tok = lambda width: pl.BlockSpec((1, tm, width), lambda bi, i: (bi, i, 0))
    vec = pl.BlockSpec((1, 1, D_MODEL), lambda bi, i: (bi, 0, 0))
    tab = pl.BlockSpec((tm, LANES), lambda bi, i: (i, 0))
    tr = lambda rows: pl.BlockSpec((1, rows, tm), lambda bi, i: (bi, 0, i))
    tok_shape = lambda width, dt: jax.ShapeDtypeStruct((b, s, width), dt)
    tr_shape = lambda rows, dt: jax.ShapeDtypeStruct((b, rows, s), dt)
    out_specs = [tok(M_WIDTH), tok(M_WIDTH), tok(M_WIDTH), tr(M_WIDTH), tr(GATE_ROWS), tr(GATE_ROWS),
                 tok(A_WIDTH), tok(KV_WIDTH), tok(KV_WIDTH), tok(2 * C_WIDTH)]
    out_shape = [tok_shape(M_WIDTH, BF16), tok_shape(M_WIDTH, BF16), tok_shape(M_WIDTH, F32), tr_shape(M_WIDTH, BF16),
                 tr_shape(GATE_ROWS, F32), tr_shape(GATE_ROWS, F32),
                 tok_shape(A_WIDTH, BF16), tok_shape(KV_WIDTH, BF16), tok_shape(KV_WIDTH, BF16),
                 tok_shape(2 * C_WIDTH, F32)]
    return pl.pallas_call(
        functools.partial(_inproj_kernel, rope=rope),
        grid=(b, s // tm),
        in_specs=[tok(D_MODEL), _const_spec((1, D_MODEL)), vec, vec,
                  _const_spec((D_MODEL, N_COLS)), _const_spec((NT_ROWS, D_MODEL)),
                  _const_spec((2 * GATE_ROWS, 1)), tab, tab],
        out_specs=out_specs,
        out_shape=out_shape,
        compiler_params=_cparams(("parallel", "parallel")),
        name="inproj_rope" if rope else "inproj_ctx",
    )(x, gain, shift, scale, w, wt, brow, cos, sin)


def _mlstm_kernel(qf_ref, kf_ref, vtf_ref, lif_ref, lff_ref, qb_ref, kb_ref, vtb_ref, lib_ref, lfb_ref,
                  c0_ref, m0_ref, hf_ref, hb_ref, cst_ref, mst_ref, *, nsub):
    t = M_CHUNK

    @pl.when(pl.program_id(1) == 0)
    def _():
        cst_ref[...] = c0_ref[...]
        mst_ref[...] = m0_ref[...]

    key = lax.broadcasted_iota(jnp.int32, (t, t), 0)
    qry = lax.broadcasted_iota(jnp.int32, (t, t), 1)
    key2 = lax.broadcasted_iota(jnp.int32, (2 * t, t), 0) % t
    qry2 = lax.broadcasted_iota(jnp.int32, (2 * t, t), 1)
    first_head_lanes = lax.broadcasted_iota(jnp.int32, (t, PAIR_W), 1) < HEAD_DIM
    first_head_rows = lax.broadcasted_iota(jnp.int32, (PAIR_W, t), 0) < HEAD_DIM
    den_r = lax.broadcasted_iota(jnp.int32, (SUBLANES, 2 * t), 0)
    den_c = lax.broadcasted_iota(jnp.int32, (SUBLANES, 2 * t), 1) // t
    st_r = lax.broadcasted_iota(jnp.int32, (STATE_ROWS, PAIR_W), 0)
    st_c = lax.broadcasted_iota(jnp.int32, (STATE_ROWS, PAIR_W), 1) // HEAD_DIM
    dirs = ((qf_ref, kf_ref, vtf_ref, lif_ref, lff_ref, hf_ref),
            (qb_ref, kb_ref, vtb_ref, lib_ref, lfb_ref, hb_ref))
    chains = [(d, p) for d in range(N_DIR) for p in range(M_PAIRS)]
    visible = [key <= qry, key >= qry]
    visible2 = [key2 <= qry2, key2 >= qry2]
    ones_st = [jnp.where(v, 1.0, 0.0).astype(BF16) for v in visible]
    state = {dp: cst_ref[0, dp[0] * M_PAIRS + dp[1]] for dp in chains}
    m_state = [mst_ref[0, d] for d in range(N_DIR)]

    gates, pairs = {}, {}
    for step in range(nsub):
        for d in range(N_DIR):
            rev = d == 1
            j = nsub - 1 - step if rev else step
            tok = slice(j * t, (j + 1) * t)
            gate_rows = slice(d * SUBLANES, (d + 1) * SUBLANES)
            li8 = dirs[d][3][0, gate_rows, tok]
            lf8 = dirs[d][4][0, gate_rows, tok]
            a8 = sum(_dot(part, ones_st[d]) for part in _split_bf16(lf8, 3))
            b8 = li8 - a8
            a_end = jnp.broadcast_to(a8[:, 0:1] if rev else a8[:, t - 1:t], (SUBLANES, t))
            g8 = a_end + b8
            g_max = jnp.broadcast_to(jnp.max(g8, axis=1, keepdims=True), (SUBLANES, t))
            b_col = jnp.transpose(jnp.concatenate([b8, jnp.zeros((t - SUBLANES, t), F32)], axis=0))
            gates[step, d] = (a8, a_end, g8, g_max)
            for p in range(M_PAIRS):
                h0, h1 = 2 * p, 2 * p + 1
                lanes = slice(p * PAIR_W, (p + 1) * PAIR_W)
                q01 = dirs[d][0][0, tok, lanes].astype(BF16)
                k01 = dirs[d][1][0, tok, lanes]
                vt01 = dirs[d][2][0, lanes, tok]
                kbd = jnp.concatenate([jnp.where(first_head_lanes, k01, 0.0),
                                       jnp.where(first_head_lanes, 0.0, k01)], axis=0).astype(BF16)
                s2 = _dot_nt(kbd, q01)
                bc = jnp.concatenate([jnp.broadcast_to(b_col[:, h0:h0 + 1], (t, t)),
                                      jnp.broadcast_to(b_col[:, h1:h1 + 1], (t, t))], axis=0)
                bc = jnp.where(visible2[d], bc, -jnp.inf)
                cm = [jnp.max(bc[i * t:(i + 1) * t], axis=0, keepdims=True) for i in range(2)]
                vbd =jnp.concatenate([jnp.where(first_head_rows, vt01, 0.0),
                                       jnp.where(first_head_rows, 0.0, vt01)], axis=1)
                den_rows = jnp.where(((den_r == h0) & (den_c == 0)) | ((den_r == h1) & (den_c == 1)), 1.0, 0.0)
                lhs = jnp.concatenate([vbd, den_rows], axis=0).astype(BF16)
                pairs[step, d, p] = (tok, lanes, q01, k01.astype(BF16), vt01, s2, bc, cm, lhs)

    row8 = lax.broadcasted_iota(jnp.int32, (SUBLANES, t), 0)
    for step in range(nsub):
        small = {}
        for d in range(N_DIR):
            a8, a_end, g8, g_max = gates[step, d]
            m_prev = m_state[d]
            m_new = jnp.maximum(a_end + m_prev, g_max)
            decay8 = jnp.exp(a_end + m_prev - m_new)
            wg8 = jnp.exp(g8 - m_new)
            m_state[d] = m_new
            small[d] = (a8, m_prev, decay8, wg8)
        weighted = {}
        for d, p in chains:
            tok, lanes, q01, k01b, vt01, s2, bc, cm, lhs = pairs[step, d, p]
            a8, m_prev = small[d][0:2]
            mx = [jnp.maximum(cm[i], m_prev[2 * p + i:2 * p + i + 1]) for i in range(2)]
            w = [jnp.exp(m_prev[2 * p + i:2 * p + i + 1] - mx[i]) for i in range(2)]
            e = [jnp.exp(-(a8[2 * p + i:2 * p + i + 1] + mx[i])) for i in range(2)]
            mr = jnp.concatenate([jnp.broadcast_to(mx[0], (t, t)), jnp.broadcast_to(mx[1], (t, t))], axis=0)
            weighted[d, p] = ((s2 * jnp.exp(bc - mr)).astype(BF16), w, e)
        for d, p in chains:
            tok, lanes, q01, k01b, vt01, s2, bc, cm, lhs = pairs[step, d, p]
            a8, m_prev, decay8, wg8 = small[d]
            sp, w, e = weighted[d, p]
            h0, h1 = 2 * p, 2 * p + 1
            st = state[d, p]
            intra = _dot(lhs, sp)
            inter = _dot_nt(st.astype(BF16), q01)
            w8 = jnp.where(row8 == h0, w[0], jnp.where(row8 == h1, w[1], 0.0))
            w_rows = jnp.concatenate([jnp.broadcast_to(w[0], (HEAD_DIM, t)), jnp.broadcast_to(w[1], (HEAD_DIM, t)),
                                      w8], axis=0)
            tot = w_rows * inter + intra
            den = jnp.concatenate([_rows(tot, PAIR_W + h0, HEAD_DIM), _rows(tot, PAIR_W + h1, HEAD_DIM)], axis=0)
            e_rows = jnp.concatenate([jnp.broadcast_to(e[0], (HEAD_DIM, t)), jnp.broadcast_to(e[1], (HEAD_DIM, t))],
                                     axis=0)
            h_t = tot[0:PAIR_W] / jnp.maximum(jnp.abs(den), e_rows)
            dirs[d][5][0, tok, lanes] = jnp.transpose(h_t)
            wg_rows = jnp.concatenate([_rows(wg8, h0, HEAD_DIM), _rows(wg8, h1, HEAD_DIM)], axis=0)
            upd = _dot(jnp.concatenate([vt01 * wg_rows, wg8], axis=0).astype(BF16), k01b)
            keep = (st_r // HEAD_DIM == st_c) | (st_r - PAIR_W == st_c + h0)
            dec = jnp.concatenate([_rows(decay8, h0, HEAD_DIM), _rows(decay8, h1, HEAD_DIM), decay8], axis=0)
            state[d, p] = dec * st + jnp.where(keep, upd, 0.0)

    for d, p in chains:
        cst_ref[0, d * M_PAIRS + p] = state[d, p]
    for d in range(N_DIR):
        mst_ref[0, d] = m_state[d]


def _mlstm(mq, mk, vt, li, lf, c0, m0, *, tb):
    b, s, _ = mq.shape
    nc = s // tb
    fwd = lambda width: pl.BlockSpec((1, tb, width), lambda bi, c: (bi, c, 0))
    bwd = lambda width: pl.BlockSpec((1, tb, width), lambda bi, c: (bi, nc - 1 - c, 0))
    fwd_t = lambda rows: pl.BlockSpec((1, rows, tb), lambda bi, c: (bi, 0, c))
    bwd_t = lambda rows: pl.BlockSpec((1, rows, tb), lambda bi, c: (bi, 0, nc - 1 - c))
    st_c = pl.BlockSpec((1, N_DIR * M_PAIRS, STATE_ROWS, PAIR_W), lambda bi, c: (bi, 0, 0, 0))
    st_m = pl.BlockSpec((1, N_DIR, SUBLANES, LANES), lambda bi, c: (bi, 0, 0, 0))
    return pl.pallas_call(
        functools.partial(_mlstm_kernel, nsub=tb // M_CHUNK),
        grid=(b, nc),
        in_specs=[fwd(M_WIDTH), fwd(M_WIDTH), fwd_t(M_WIDTH), fwd_t(GATE_ROWS), fwd_t(GATE_ROWS),
                  bwd(M_WIDTH), bwd(M_WIDTH), bwd_t(M_WIDTH), bwd_t(GATE_ROWS), bwd_t(GATE_ROWS),
                  st_c, st_m],
        out_specs=[fwd(M_WIDTH), bwd(M_WIDTH), st_c, st_m],
        out_shape=[jax.ShapeDtypeStruct((b, s, M_WIDTH), F32), jax.ShapeDtypeStruct((b, s, M_WIDTH), F32),
                   jax.ShapeDtypeStruct(c0.shape, F32), jax.ShapeDtypeStruct(m0.shape, F32)],
        compiler_params=_cparams(("parallel", "arbitrary")),
        name="mlstm_scan",
    )(mq, mk, vt, li, lf, mq, mk, vt, li, lf, c0, m0)


def _attn_kernel(*refs, seq, nq, local):
    tq = WINDOW
    if local:
        q_ref, kp_ref, kc_ref, kn_ref, vp_ref, vc_ref, vn_ref, kx_ref, vx_ref, sink_ref, o_ref = refs
        k_blocks = [kp_ref[0]] + [kc_ref[0, i * tq:(i + 1) * tq] for i in range(nq)] + [kn_ref[0]]
        v_blocks = [vp_ref[0]] + [vc_ref[0, i * tq:(i + 1) * tq] for i in range(nq)] + [vn_ref[0]]
        start = pl.program_id(1) * (nq * tq)
        row = lax.broadcasted_iota(jnp.int32, (tq, 2 * tq), 0)
        side = lax.broadcasted_iota(jnp.int32, (tq, 2 * tq), 1)
    else:
        q_ref, kx_ref, vx_ref, sink_ref, o_ref = refs
    n_keys = kx_ref.shape[1] + (3 * tq if local else 0)
    kv_of_lane = lax.broadcasted_iota(jnp.int32, (1, KV_WIDTH), 1) // HEAD_DIM
    sel = [jnp.where(kv_of_lane == g, 1.0, 0.0).astype(BF16) for g in range(A_KV_HEADS)]
    first_kv_q =lax.broadcasted_iota(jnp.int32, (A_GROUP * tq, KV_WIDTH), 1) < HEAD_DIM
    scores, sinks, values = [], [], []
    for blk in range(nq):
        if local:
            k_cat = jnp.concatenate([k_blocks[blk], k_blocks[blk + 2], k_blocks[blk + 1], kx_ref[0]], axis=0)
            v_cat = jnp.concatenate([v_blocks[blk], v_blocks[blk + 2], v_blocks[blk + 1], vx_ref[0]], axis=0)
            qpos = start + blk * tq + row
            kpos = start + blk * tq + jnp.where(side < tq, side - tq, side)
            ok = (jnp.abs(qpos - kpos) <= WINDOW) & (kpos >= 0) & (kpos < seq)
            bias = jnp.concatenate([jnp.where(ok, 0.0, NEG_INF)] * A_GROUP, axis=0)
        else:
            k_cat, v_cat = kx_ref[0], vx_ref[0]
        k_bd = jnp.concatenate([k_cat * sel[0], k_cat * sel[1]], axis=0)
        v_bd = jnp.concatenate(
            [jnp.concatenate([v_cat * sel[g], jnp.broadcast_to(sel[g], v_cat.shape)], axis=1)
             for g in range(A_KV_HEADS)], axis=0)
        qs = jnp.concatenate([q_ref[0, blk * tq:(blk + 1) * tq, j * KV_WIDTH:(j + 1) * KV_WIDTH]
                              for j in range(A_GROUP)], axis=0)
        s = _dot_nt(qs, k_bd)
        if local:
            s = jnp.concatenate([s[:, :2 * tq] + bias, s[:, 2 * tq:n_keys],
                                 s[:, n_keys:n_keys + 2 * tq] + bias, s[:, n_keys + 2 * tq:]], axis=1)
        scores.append(s)
        values.append(v_bd)
        sinks.append([jnp.concatenate(
            [jnp.broadcast_to(sink_ref[0:1, g * A_GROUP + j:g * A_GROUP + j + 1], (tq, KV_WIDTH))
             for j in range(A_GROUP)], axis=0) for g in range(A_KV_HEADS)])
    maxes = [[jnp.maximum(jnp.broadcast_to(jnp.max(s[:, g * n_keys:(g + 1) * n_keys], axis=1, keepdims=True),
                                           sk[g].shape), sk[g]) for g in range(A_KV_HEADS)]
             for s, sk in zip(scores, sinks)]
    probs = [jnp.concatenate([jnp.exp2(s[:, i:i + LANES] - m[i // n_keys]).astype(BF16)
                              for i in range(0, 2 * n_keys, LANES)], axis=1) for s, m in zip(scores, maxes)]
    outs = [_dot(p, v_bd) for p, v_bd in zip(probs, values)]
    for blk, (ol, sk, m) in enumerate(zip(outs, sinks, maxes)):
        sink_term = jnp.exp2(jnp.where(first_kv_q, sk[0] - m[0], sk[1] - m[1]))
        o = (ol[:, :KV_WIDTH] / (ol[:, KV_WIDTH:] + sink_term)).astype(o_ref.dtype)
        for j in range(A_GROUP):
            o_ref[0, blk * tq:(blk + 1) * tq, j * KV_WIDTH:(j + 1) * KV_WIDTH] = o[j * tq:(j + 1) * tq]


def _attention(aq, ak, av, kx, vx, sink, *, local):
    b, s, _ = aq.shape
    n_ctx = kx.shape[1]
    tq = WINDOW
    nb = s // tq
    nq = min(8, nb)
    qs = pl.BlockSpec((1, nq * tq, A_WIDTH), lambda bi, i: (bi, i, 0))
    cx = pl.BlockSpec((1, n_ctx, KV_WIDTH), lambda bi, i: (bi, 0, 0))
    sk = pl.BlockSpec((1, A_HEADS), lambda bi, i: (0, 0))
    if local:
        prv = pl.BlockSpec((1, tq, KV_WIDTH), lambda bi, i: (bi, jnp.maximum(nq * i - 1, 0), 0))
        cur = pl.BlockSpec((1, nq * tq, KV_WIDTH), lambda bi, i: (bi, i, 0))
        nxt = pl.BlockSpec((1, tq, KV_WIDTH), lambda bi, i: (bi, jnp.minimum(nq * (i + 1), nb - 1), 0))
        in_specs = [qs, prv, cur, nxt, prv, cur, nxt, cx, cx, sk]
        args = (aq, ak, ak, ak, av, av, av, kx, vx, sink)
    else:
        in_specs = [qs, cx, cx, sk]
        args = (aq, kx, vx, sink)
    return pl.pallas_call(
        functools.partial(_attn_kernel, seq=s, nq=nq, local=local),
        grid=(b, nb // nq),
        in_specs=in_specs,
        out_specs=qs,
        out_shape=jax.ShapeDtypeStruct((b, s, A_WIDTH), BF16),
        compiler_params=_cparams(("parallel", "parallel")),
        name="window_attn" if local else "ctx_attn",
    )(*args)


def _conv_prepare(up_ref, uc_ref, un_ref, ybuf, first, last):
    tc = uc_ref.shape[1]
    span = tc + 2 * CONV_HALO - SUBLANES

    def glu(u):
        return u[:, :C_WIDTH] * _sigmoid(u[:, C_WIDTH:])

    ybuf[0, 0:CONV_HALO] = jnp.where(first, 0.0, glu(up_ref[0]))
    ybuf[0, CONV_HALO:CONV_HALO + tc] = glu(uc_ref[0])
    ybuf[0, CONV_HALO + tc:2 * CONV_HALO + tc] = jnp.where(last, 0.0, glu(un_ref[0]))
    for r in range(1, SUBLANES):
        ybuf[r, 0:span] = ybuf[0, r:r + span]


def _conv_rows(r0, sub, dw_ref, db_ref, lg_ref, lb_ref, pw_ref, ybuf, c_ref):
    off = CONV_HALO - CONV_K // 2
    acc = jnp.zeros((sub // SUBLANES, SUBLANES, C_WIDTH), F32)
    for k in range(CONV_K):
        r = (off + k) % SUBLANES
        base = r0 + off + k - r
        acc = acc + ybuf[r, base:base + sub, :].reshape(sub // SUBLANES, SUBLANES, C_WIDTH) * dw_ref[k]
    y = acc.reshape(sub, C_WIDTH) + db_ref[...]
    mu = jnp.mean(y, axis=-1, keepdims=True)
    yc = y - mu
    var = jnp.mean(yc * yc, axis=-1, keepdims=True)
    z = yc * lax.rsqrt(var + EPS) * lg_ref[...] + lb_ref[...]
    z = z * _sigmoid(z)
    c_ref[r0:r0 + sub, :] = _dot(z.astype(BF16), pw_ref[...]).astype(c_ref.dtype)


def _mix_ffn_kernel(*refs, final, nt):
    (x_ref, hf_ref, hb_ref, mo_ref, a_ref, up_ref, uc_ref, un_ref, dw_ref, db_ref, lg_ref, lb_ref, pw_ref,
     hg_ref, wo_ref, g1_ref, sh2_ref, sc2_ref, g2_ref, gain2_ref, wi_ref, wf_ref) = refs[:22]
    fg_ref = refs[22] if final else None
    o_ref, ybuf, c_scr = refs[-3:]
    i = pl.program_id(0)

    @pl.when(i == 0)
    def _():
        c_scr[...] = jnp.zeros(c_scr.shape, c_scr.dtype)

    hs = hf_ref[0] + hb_ref[0]
    ri = lax.broadcasted_iota(jnp.int32, (M_WIDTH, M_WIDTH), 0) // HEAD_DIM
    ci = lax.broadcasted_iota(jnp.int32, (M_WIDTH, M_WIDTH), 1) // HEAD_DIM
    same_head = jnp.where(ri == ci, 1.0, 0.0).astype(BF16)
    ms = sum(_dot(part, same_head) for part in _split_bf16(hs * hs, 2)) * (1.0 / HEAD_DIM)
    mx = hs * lax.rsqrt(ms + EPS) * hg_ref[...] * _sigmoid(mo_ref[0])
    y = (_dot(mx.astype(BF16), wo_ref[0:M_WIDTH, :])
         + _dot(a_ref[0], wo_ref[M_WIDTH:M_WIDTH + A_WIDTH, :])
         + _dot(c_scr[...], wo_ref[M_WIDTH + A_WIDTH:, :]))
    x1 = x_ref[0] + g1_ref[0] * y
    h2 = _rms(x1) * (gain2_ref[...] * (1.0 + sc2_ref[0])) + sh2_ref[0]
    hb2 = h2.astype(BF16)

    tile = jnp.minimum(i, pl.num_programs(0) - 2) % nt
    tm = uc_ref.shape[1]
    sub = tm // CONV_PIECES
    conv_rows = functools.partial(_conv_rows, sub=sub, dw_ref=dw_ref, db_ref=db_ref, lg_ref=lg_ref, lb_ref=lb_ref,
                                  pw_ref=pw_ref, ybuf=ybuf, c_ref=c_scr)
    _conv_prepare(up_ref, uc_ref, un_ref, ybuf, tile == 0, tile == nt - 1)
    hid = []
    for c in range(FFN_CHUNKS):
        lo, hi = c * (FFN_HIDDEN // FFN_CHUNKS), (c + 1) * (FFN_HIDDEN // FFN_CHUNKS)
        gate = _dot(hb2, wi_ref[:, lo:hi])
        up = _dot(hb2, wi_ref[:, FFN_HIDDEN + lo:FFN_HIDDEN + hi])
        hid.append((gate * _sigmoid(gate) * up).astype(BF16))
        if c < CONV_PIECES:
            conv_rows(c * sub)
    x2 = x1 + g2_ref[0] * _dot(jnp.concatenate(hid, axis=1), wf_ref[...])
    if final:
        x2 = _rms(x2) * fg_ref[...]
    o_ref[0] = x2


def _mix_ffn(x, hf, hb, mo, a, cu, conv_w, hg, wo, g1, sh2, sc2, g2, gain2, wi, wf, fg, *, tm, layer):
    b, s, _ = x.shape
    final = fg is not None
    nt = s // tm
    n = b * nt
    nh = s // CONV_HALO
    r = tm // CONV_HALO
    cur = lambda i: jnp.maximum(i - 1, 0)
    nxt = lambda i: jnp.minimum(i, n - 1)
    tok = lambda width: pl.BlockSpec((1, tm, width), lambda i: (cur(i) // nt, cur(i) % nt, 0))
    vec = pl.BlockSpec((1, 1, D_MODEL), lambda i: (cur(i) // nt, 0, 0))
    u_cur = pl.BlockSpec((1, tm, 2 * C_WIDTH), lambda i: (nxt(i) // nt, nxt(i) % nt, 0))
    u_prv = pl.BlockSpec((1, CONV_HALO, 2 * C_WIDTH),
                         lambda i: (nxt(i) // nt, jnp.maximum((nxt(i) % nt) * r - 1, 0), 0))
    u_nxt = pl.BlockSpec((1, CONV_HALO, 2 * C_WIDTH),
                         lambda i: (nxt(i) // nt, jnp.minimum((nxt(i) % nt + 1) * r, nh - 1), 0))
    dw, db, lg, lb, pw = conv_w
    in_specs = [tok(D_MODEL), tok(M_WIDTH), tok(M_WIDTH), tok(M_WIDTH), tok(A_WIDTH), u_prv, u_cur, u_nxt,
                _const_spec((CONV_K + 1, SUBLANES, C_WIDTH)), _const_spec((1, C_WIDTH)), _const_spec((1, C_WIDTH)),
                _const_spec((1, C_WIDTH)), _const_spec((C_WIDTH, C_WIDTH)),
                _const_spec((1, M_WIDTH)), _const_spec((D_MODEL, D_MODEL)), vec, vec, vec, vec,
                _const_spec((1, D_MODEL)), _layer_spec((D_MODEL, 2 * FFN_HIDDEN), layer),
                _layer_spec((FFN_HIDDEN, D_MODEL), layer)]
    args = [x, hf, hb, mo, a, cu, cu, cu, dw, db, lg, lb, pw, hg, wo, g1, sh2, sc2, g2, gain2, wi, wf]
    if final:
        in_specs.append(_const_spec((1, D_MODEL)))
        args.append(fg)
    return pl.pallas_call(
        functools.partial(_mix_ffn_kernel, final=final, nt=nt),
        grid=(n + 1,),
        in_specs=in_specs,
        out_specs=tok(D_MODEL),
        out_shape=jax.ShapeDtypeStruct((b, s, D_MODEL), F32),
        scratch_shapes=[pltpu.VMEM((SUBLANES, tm + 2 * CONV_HALO, C_WIDTH), F32), pltpu.VMEM((tm, C_WIDTH), BF16)],
        compiler_params=_cparams(("arbitrary",)),
        name="mix_ffn_final" if final else "mix_ffn",
    )(*args)


def _rope_tables(seq):
    half = HEAD_DIM // 4
    freqs = ROPE_THETA ** (-jnp.arange(half, dtype=F32) / half)
    t = jnp.arange(seq)
    sign = jnp.concatenate([-jnp.ones((half,), F32), jnp.ones((half,), F32)])
    cos_parts, sin_parts = [], []
    for pos in (t // GRID_W, t % GRID_W):
        ang = pos.astype(F32)[:, None] * freqs[None, :]
        c, s = jnp.cos(ang), jnp.sin(ang)
        cos_parts.append(jnp.concatenate([c, c], axis=-1))
        sin_parts.append(jnp.concatenate([s, s], axis=-1) * sign)
    cos = jnp.concatenate(cos_parts, axis=-1)
    sin = jnp.concatenate(sin_parts, axis=-1)
    rep = LANES // HEAD_DIM
    return jnp.tile(cos, (1, rep)), jnp.tile(sin, (1, rep))


def _layer_weights(w_in, gate_bias):
    o = np.cumsum((0,) + IN_SPLITS)
    col = lambda i: w_in[:, int(o[i]):int(o[i + 1])]
    scale = HEAD_DIM ** -0.5
    aq = (col(5) * (scale * LOG2E)).reshape(D_MODEL, A_HEADS, HEAD_DIM)[:, A_HEAD_ORDER].reshape(D_MODEL, A_WIDTH)
    w = jnp.concatenate([col(0), col(1) * scale, col(3), aq, col(6), col(7), col(8)], axis=1).astype(BF16)
    gates = col(4).T.reshape(N_DIR, 2, M_HEADS, D_MODEL)
    pad = ((0, 0), (0, SUBLANES - M_HEADS), (0, 0))
    li_rows = jnp.pad(gates[:, 0], pad).reshape(GATE_ROWS, D_MODEL)
    lf_rows = jnp.pad(gates[:, 1], pad).reshape(GATE_ROWS, D_MODEL)
    wt = jnp.concatenate([col(2).T, li_rows, lf_rows], axis=0).astype(BF16)
    bias = jnp.pad(gate_bias.astype(F32), ((0, 0), (0, 0), (0, SUBLANES - M_HEADS)))
    brow = jnp.concatenate([bias[:, 0].reshape(GATE_ROWS), bias[:, 1].reshape(GATE_ROWS)]).reshape(2 * GATE_ROWS, 1)
    return w, wt, brow


def kernel(x, c, ctx, c_ctx, w_mod, b_mod, norm_gain, w_in, mlstm_gate_bias, mlstm_head_gain, attn_sink,
           conv_dw_w, conv_dw_b, conv_ln_g, conv_ln_b, conv_pw_w, w_out, w_ffn_in, w_ffn_out, final_gain):
    b, seq, _ = x.shape
    n_ctx = ctx.shape[1]
    rows = -(-(b + 1) // SUBLANES) * SUBLANES
    cc = jnp.zeros((rows, D_MODEL), F32).at[:b].set(c).at[b].set(c_ctx)
    mod = _modulation(cc, w_mod, b_mod)
    cos_x, sin_x = _rope_tables(seq)
    cos_c, sin_c = cos_x[:n_ctx], sin_x[:n_ctx]
    zero_c = jnp.zeros((b, N_DIR * M_PAIRS, STATE_ROWS, PAIR_W), F32)
    zero_m = jnp.zeros((b, N_DIR, SUBLANES, LANES), F32)
    wi = _to_bf16(w_ffn_in, 256)
    wf = _to_bf16(w_ffn_out, FFN_HIDDEN // 8)
    tm_in = min(1024, seq)
    tb_x = min(2048, seq)
    tm_x = min(512, seq)
    tm_c = tb_c = min(256, n_ctx)

    for l in range(DEPTH):
        last = l == DEPTH - 1
        mx_ = mod[l, :b].reshape(b, 1, 6 * D_MODEL)
        mc_ = jnp.broadcast_to(mod[l, b].reshape(1, 1, 6 * D_MODEL), (b, 1, 6 * D_MODEL))
        part = lambda m, i: m[:, :, i * D_MODEL:(i + 1) * D_MODEL]
        w, wt, brow = _layer_weights(w_in[l], mlstm_gate_bias[l])
        gain1 = norm_gain[l, 0].reshape(1, D_MODEL)
        gain2 = norm_gain[l, 1].reshape(1, D_MODEL)
        hg = mlstm_head_gain[l].reshape(1, M_WIDTH)
        sink = attn_sink[l].reshape(1, A_HEADS).astype(F32) * LOG2E
        dw = jnp.broadcast_to(jnp.pad(conv_dw_w[l], ((0, 1), (0, 0)))[:, None, :], (CONV_K + 1, SUBLANES, C_WIDTH))
        db = conv_dw_b[l].reshape(1, C_WIDTH)
        lg = conv_ln_g[l].reshape(1, C_WIDTH)
        lb = conv_ln_b[l].reshape(1, C_WIDTH)
        pw = conv_pw_w[l].astype(BF16)
        wo_a = w_out[l, M_WIDTH:M_WIDTH + A_WIDTH].reshape(A_HEADS, HEAD_DIM, D_MODEL)[A_HEAD_ORDER, :, :]
        wo = jnp.concatenate([w_out[l, :M_WIDTH], wo_a.reshape(A_WIDTH, D_MODEL), w_out[l, M_WIDTH + A_WIDTH:]],
                             axis=0).astype(BF16)

        px = _inproj(x, gain1, part(mx_, 0), part(mx_, 1), w, wt, brow, cos_x, sin_x, rope=True, tm=tm_in)
        pc = _inproj(ctx, gain1, part(mc_, 0), part(mc_, 1), w, wt, brow, cos_c, sin_c, rope=False, tm=tm_c)
        mq, mk, mo, vt, li, lf, aq, ak, av, cu = px
        cmq, cmk, cmo, cvt, cli, clf, caq, cak, cav, ccu = pc

        chf, chb, cst, mst = _mlstm(cmq, cmk, cvt, cli, clf, zero_c, zero_m, tb=tb_c)
        hf, hb, _, _ = _mlstm(mq, mk, vt, li, lf, cst, mst, tb=tb_x)
        a_x = _attention(aq, ak, av, cak, cav, sink, local=True)
        conv_w = (dw, db, lg, lb, pw)
        x = _mix_ffn(x, hf, hb, mo, a_x, cu, conv_w, hg, wo, part(mx_, 2), part(mx_, 3), part(mx_, 4), part(mx_, 5),
                     gain2, wi, wf, final_gain.reshape(1, D_MODEL) if last else None, tm=tm_x, layer=l)
        if not last:
            a_c = _attention(caq, cak, cav, cak, cav, sink, local=False)
            ctx = _mix_ffn(ctx, chf, chb, cmo, a_c, ccu, conv_w, hg, wo, part(mc_, 2), part(mc_, 3), part(mc_, 4),
                           part(mc_, 5), gain2, wi, wf, None, tm=tm_c, layer=l)
    return x
```
